```python
import math
import jax, jax.numpy as jnp
from jax import lax
import numpy as np

D_MODEL = 1024
BATCH = 2
SEQ = 16384
DEPTH = 4

N_MIXERS = 3
LAYER_MIXER = tuple(i % N_MIXERS for i in range(DEPTH))
N_S5 = LAYER_MIXER.count(0)
N_HG = LAYER_MIXER.count(1)
N_MLA = LAYER_MIXER.count(2)
MIX_WIDTH = D_MODEL
S5_GROUP = 16
S5_GROUPS = MIX_WIDTH // S5_GROUP
S5_STATE = 64
S5_DT_MIN = 0.001
S5_DT_MAX = 0.1
S5_BLOCK = 1024
HG_HEAD_DIM = 128
HG_HEADS = MIX_WIDTH // HG_HEAD_DIM
HG_CHUNK = 64
MLA_HEADS = 8
MLA_NOPE = 128
MLA_ROPE = 64
MLA_V = 128
MLA_Q_LORA = 384
MLA_KV_LORA = 256
ROPE_THETA = 10000.0
ATTN_BLOCK = 128
MAX_POS_OFFSET = 4096
FFN_HIDDEN = -(-8 * D_MODEL // (3 * 256)) * 256
DEEPNORM_ALPHA = (2 * DEPTH) ** 0.25
DEEPNORM_BETA = (8 * DEPTH) ** -0.25
LN_EPS = 1e-5
RMS_EPS = 1e-6

kernel_name = 'hybrid_s5_hgrn2_mla_deepnorm_encoder'


def layer_norm(x, g, b):
    xf = x.astype(jnp.float32)
    mu = jnp.mean(xf, axis=-1, keepdims=True)
    var = jnp.mean(jnp.square(xf - mu), axis=-1, keepdims=True)
    return ((xf - mu) * lax.rsqrt(var + LN_EPS) * g + b).astype(x.dtype)


def rms_norm(x, g):
    xf = x.astype(jnp.float32)
    return (xf * lax.rsqrt(jnp.mean(xf * xf, axis=-1, keepdims=True) + RMS_EPS) * g).astype(x.dtype)


def swiglu(x, w_in, w_out):
    gate, up = jnp.split(x @ w_in, 2, axis=-1)
    return (jax.nn.silu(gate) * up) @ w_out


def s5_direction(u, lam_re, lam_im, log_step, b_re, b_im, c_re, c_im):
    seq, bsz = u.shape[0], u.shape[1]
    blk = math.gcd(seq, S5_BLOCK)
    f32 = jnp.float32
    lam = lax.complex(lam_re.astype(f32), lam_im.astype(f32))
    step = jnp.exp(log_step.astype(f32))[:, None]
    lam_dt = lam * step
    lam_bar = jnp.exp(lam_dt)
    b = lax.complex(b_re.astype(f32), b_im.astype(f32))
    b_bar = ((lam_bar - 1.0) / lam)[..., None] * b
    c = lax.complex(c_re.astype(f32), c_im.astype(f32))
    powers = jnp.exp(lam_dt[None] * jnp.arange(1, blk + 1, dtype=f32)[:, None, None])
    a_blk = jnp.broadcast_to(lam_bar, (blk, 1) + lam_bar.shape)
    u_blocks = u.astype(f32).reshape(seq // blk, blk, bsz, S5_GROUPS, S5_GROUP)

    def combine(left, right):
        a_l, b_l = left
        a_r, b_r = right
        return a_r * a_l, a_r * b_l + b_r

    def step_fn(h, u_blk):
        bu = jnp.einsum('lbgh,gph->lbgp', u_blk.astype(jnp.complex64), b_bar)
        _, h_loc = lax.associative_scan(combine, (a_blk, bu), axis=0)
        h_all = h_loc + powers[:, None] * h[None]
        y = jnp.einsum('ghp,lbgp->lbgh', c, h_all).real
        return h_all[-1], y

    h0 = jnp.zeros((bsz, S5_GROUPS, S5_STATE), jnp.complex64)
    _, y = lax.scan(step_fn, h0, u_blocks)
    return y.reshape(seq, bsz, S5_GROUPS, S5_GROUP)


def s5_mixer(x, w_in, lam_re, lam_im, log_step, b_re, b_im, c_re, c_im, d_skip, w_glu, b_glu, w_out):
    bsz, seq, _ = x.shape
    u = x @ w_in
    u_s = jnp.swapaxes(u, 0, 1).reshape(seq, bsz, S5_GROUPS, S5_GROUP)
    y_fw = s5_direction(u_s, lam_re[0], lam_im[0], log_step[0], b_re[0], b_im[0], c_re[0], c_im[0])
    y_bw = jnp.flip(s5_direction(jnp.flip(u_s, 0), lam_re[1], lam_im[1], log_step[1],
                                 b_re[1], b_im[1], c_re[1], c_im[1]), 0)
    y = jnp.swapaxes((y_fw + y_bw).reshape(seq, bsz, MIX_WIDTH), 0, 1).astype(x.dtype) + d_skip * u
    y = jax.nn.gelu(y)
    y = y * jax.nn.sigmoid(y @ w_glu + b_glu)
    return y @ w_out


def chunked_gated_recurrence(q, k, v, log_f):
    n, h, seq, dk = q.shape
    dv = v.shape[-1]
    nc = seq // HG_CHUNK

    def to_chunks(t):
        return jnp.moveaxis(t.reshape(n, h, nc, HG_CHUNK, t.shape[-1]), 2, 0)

    lower = jnp.tril(jnp.ones((HG_CHUNK, HG_CHUNK), bool))[:, :, None]

    def step(state, inp):
        qc, kc, vc, lfc = inp
        b = jnp.cumsum(lfc, axis=2)
        b_last = b[:, :, -1:, :]
        o_inter = jnp.einsum('nhtk,nhkv->nhtv', qc * jnp.exp(b), state)
        diff = b[:, :, :, None, :] - b[:, :, None, :, :]
        decay = jnp.where(lower, jnp.exp(jnp.where(lower, diff, 0.0)), 0.0)
        scores = jnp.einsum('nhtk,nhsk,nhtsk->nhts', qc, kc, decay)
        o_intra = jnp.einsum('nhts,nhsv->nhtv', scores, vc)
        new_state = (jnp.exp(b_last[:, :, 0, :])[..., None] * state
                     + jnp.einsum('nhsk,nhsv->nhkv', kc * jnp.exp(b_last - b), vc))
        return new_state, o_inter + o_intra

    s0 = jnp.zeros((n, h, dk, dv), jnp.float32)
    _, o = lax.scan(step, s0, (to_chunks(q), to_chunks(k), to_chunks(v), to_chunks(log_f)))
    return jnp.moveaxis(o, 0, 2).reshape(n, h, seq, dv)


def hgrn2_mixer(x, w_in, lower_bound, norm_g, w_out):
    bsz, seq, _ = x.shape
    q, f_fw, f_bw, i_in, g = jnp.split(x @ w_in, 5, axis=-1)

    def heads(t):
        return t.reshape(bsz, seq, HG_HEADS, HG_HEAD_DIM).transpose(0, 2, 1, 3).astype(jnp.float32)

    q = jax.nn.silu(heads(q))
    v = heads(i_in)
    lb = lower_bound.reshape(HG_HEADS, 1, HG_HEAD_DIM).astype(jnp.float32)

    def gates(f):
        z = heads(f)
        log_f = jnp.logaddexp(jnp.log(lb), jnp.log1p(-lb) + jax.nn.log_sigmoid(z))
        k = (1.0 - lb) * jax.nn.sigmoid(-z)
        return log_f, k

    lf_fw, k_fw = gates(f_fw)
    lf_bw, k_bw = gates(f_bw)
    flip = lambda t: jnp.flip(t, axis=2)
    q2 = jnp.concatenate([q, flip(q)], axis=0)
    k2 = jnp.concatenate([k_fw, flip(k_bw)], axis=0)
    v2 = jnp.concatenate([v, flip(v)], axis=0)
    lf2 = jnp.concatenate([lf_fw, flip(lf_bw)], axis=0)
    o = chunked_gated_recurrence(q2, k2, v2, lf2)
    o = o[:bsz] + flip(o[bsz:])
    o = rms_norm(o, norm_g)
    o = o.transpose(0, 2, 1, 3).reshape(bsz, seq, MIX_WIDTH).astype(x.dtype)
    o = o * jax.nn.silu(g)
    return o @ w_out


def rope_cos_sin(positions):
    half = MLA_ROPE // 2
    inv_freq = 1.0 / (ROPE_THETA ** (jnp.arange(half, dtype=jnp.float32) * (2.0 / MLA_ROPE)))
    ang = positions.astype(jnp.float32)[..., None] * inv_freq
    return jnp.cos(ang), jnp.sin(ang)


def apply_rope(t, cos, sin):
    half = t.shape[-1] // 2
    t1, t2 = t[..., :half], t[..., half:]
    return jnp.concatenate([t1 * cos - t2 * sin, t1 * sin + t2 * cos], axis=-1).astype(t.dtype)


def dense_attention(q_nope, q_pe, k_nope, k_pe, v):
    bsz, seq, h, _ = q_nope.shape
    nb = seq // ATTN_BLOCK
    qn_b = jnp.moveaxis(q_nope.reshape(bsz, nb, ATTN_BLOCK, h, MLA_NOPE), 1, 0)
    qp_b = jnp.moveaxis(q_pe.reshape(bsz, nb, ATTN_BLOCK, h, MLA_ROPE), 1, 0)

    def one_block(blk):
        qn, qp = blk
        s = (jnp.einsum('bqhd,bkhd->bhqk', qn, k_nope, preferred_element_type=jnp.float32)
             + jnp.einsum('bqhr,bkr->bhqk', qp, k_pe, preferred_element_type=jnp.float32))
        p = jax.nn.softmax(s, axis=-1).astype(v.dtype)
        return jnp.einsum('bhqk,bkhd->bqhd', p, v)

    o = lax.map(one_block, (qn_b, qp_b))
    return jnp.moveaxis(o, 0, 1).reshape(bsz, seq, h, MLA_V)


def mla_mixer(x, positions, w_in, q_norm_g, w_q_b, kv_norm_g, w_kv_b, w_out):
    bsz, seq, _ = x.shape
    q_lat, kv_lat, k_rope = jnp.split(x @ w_in, [MLA_Q_LORA, MLA_Q_LORA + MLA_KV_LORA], axis=-1)
    q = (rms_norm(q_lat, q_norm_g) @ w_q_b).reshape(bsz, seq, MLA_HEADS, MLA_NOPE + MLA_ROPE)
    kv = (rms_norm(kv_lat, kv_norm_g) @ w_kv_b).reshape(bsz, seq, MLA_HEADS, MLA_NOPE + MLA_V)
    cos, sin = rope_cos_sin(positions)
    scale = (MLA_NOPE + MLA_ROPE) ** -0.5
    q_nope = q[..., :MLA_NOPE] * scale
    q_pe = apply_rope(q[..., MLA_NOPE:], cos[:, :, None, :], sin[:, :, None, :]) * scale
    k_pe = apply_rope(k_rope, cos, sin)
    k_nope, v = kv[..., :MLA_NOPE], kv[..., MLA_NOPE:]
    o = dense_attention(q_nope, q_pe, k_nope, k_pe, v)
    return o.reshape(bsz, seq, MLA_HEADS * MLA_V) @ w_out


def setup_inputs(seed: int = 0) -> dict:
    key = jax.random.key(seed)
    ks = iter(jax.random.split(key, 40))
    f32 = jnp.float32
    d, w = D_MODEL, MIX_WIDTH
    g, p, hs = S5_GROUPS, S5_STATE, S5_GROUP

    def nrm(shape, scale):
        return jax.random.normal(next(ks), shape, f32) * scale

    inp = {}
    inp['x'] = nrm((BATCH, SEQ, d), 1.0)
    inp['positions'] = (jnp.arange(SEQ, dtype=jnp.int32)[None, :]
                        + jax.random.randint(next(ks), (BATCH, 1), 0, MAX_POS_OFFSET, dtype=jnp.int32))
    inp['ln_mix_g'] = 1.0 + nrm((DEPTH, d), 0.02)
    inp['ln_mix_b'] = nrm((DEPTH, d), 0.01)
    inp['ln_ffn_g'] = 1.0 + nrm((DEPTH, d), 0.02)
    inp['ln_ffn_b'] = nrm((DEPTH, d), 0.01)
    inp['ffn_w_in'] = nrm((DEPTH, d, 2 * FFN_HIDDEN), d ** -0.5)
    inp['ffn_w_out'] = nrm((DEPTH, FFN_HIDDEN, d), DEEPNORM_BETA * FFN_HIDDEN ** -0.5)
    inp['s5_w_in'] = nrm((N_S5, d, w), d ** -0.5)
    inp['s5_lam_re'] = -0.5 + nrm((N_S5, 2, g, p), 0.01)
    inp['s5_lam_im'] = jnp.pi * jnp.arange(p, dtype=f32) + nrm((N_S5, 2, g, p), 0.01)
    inp['s5_log_step'] = jax.random.uniform(next(ks), (N_S5, 2, g), f32,
                                            math.log(S5_DT_MIN), math.log(S5_DT_MAX))
    inp['s5_b_re'] = nrm((N_S5, 2, g, p, hs), (2 * hs) ** -0.5)
    inp['s5_b_im'] = nrm((N_S5, 2, g, p, hs), (2 * hs) ** -0.5)
    inp['s5_c_re'] = nrm((N_S5, 2, g, hs, p), (2 * p) ** -0.5)
    inp['s5_c_im'] = nrm((N_S5, 2, g, hs, p), (2 * p) ** -0.5)
    inp['s5_d'] = nrm((N_S5, w), 1.0)
    inp['s5_w_glu'] = nrm((N_S5, w, w), w ** -0.5)
    inp['s5_b_glu'] = nrm((N_S5, w), 0.01)
    inp['s5_w_out'] = nrm((N_S5, w, d), DEEPNORM_BETA * w ** -0.5)
    inp['hg_w_in'] = nrm((N_HG, d, 5 * w), d ** -0.5)
    inp['hg_lower_bound'] = nrm((DEPTH, w), 0.1)
    inp['hg_norm_g'] = 1.0 + nrm((N_HG, HG_HEAD_DIM), 0.02)
    inp['hg_w_out'] = nrm((N_HG, w, d), DEEPNORM_BETA * w ** -0.5)
    inp['mla_w_in'] = nrm((N_MLA, d, MLA_Q_LORA + MLA_KV_LORA + MLA_ROPE), d ** -0.5)
    inp['mla_q_norm_g'] = 1.0 + nrm((N_MLA, MLA_Q_LORA), 0.02)
    inp['mla_w_q_b'] = nrm((N_MLA, MLA_Q_LORA, MLA_HEADS * (MLA_NOPE + MLA_ROPE)), MLA_Q_LORA ** -0.5)
    inp['mla_kv_norm_g'] = 1.0 + nrm((N_MLA, MLA_KV_LORA), 0.02)
    inp['mla_w_kv_b'] = nrm((N_MLA, MLA_KV_LORA, MLA_HEADS * (MLA_NOPE + MLA_V)), MLA_KV_LORA ** -0.5)
    inp['mla_w_out'] = nrm((N_MLA, MLA_HEADS * MLA_V, d), DEEPNORM_BETA * (MLA_HEADS * MLA_V) ** -0.5)
    return inp


def reference(x, positions, ln_mix_g, ln_mix_b, ln_ffn_g, ln_ffn_b, ffn_w_in, ffn_w_out,
              s5_w_in, s5_lam_re, s5_lam_im, s5_log_step, s5_b_re, s5_b_im, s5_c_re, s5_c_im,
              s5_d, s5_w_glu, s5_b_glu, s5_w_out,
              hg_w_in, hg_lower_bound, hg_norm_g, hg_w_out,
              mla_w_in, mla_q_norm_g, mla_w_q_b, mla_kv_norm_g, mla_w_kv_b, mla_w_out):
    lbs = jax.nn.softmax(hg_lower_bound.astype(jnp.float32), axis=0)
    lbs = jnp.cumsum(lbs, axis=0) - lbs[0]
    h = x
    for layer in range(DEPTH):
        kind = LAYER_MIXER[layer]
        slot = layer // N_MIXERS
        if kind == 0:
            m = s5_mixer(h, s5_w_in[slot], s5_lam_re[slot], s5_lam_im[slot], s5_log_step[slot],
                         s5_b_re[slot], s5_b_im[slot], s5_c_re[slot], s5_c_im[slot],
                         s5_d[slot], s5_w_glu[slot], s5_b_glu[slot], s5_w_out[slot])
        elif kind == 1:
            m = hgrn2_mixer(h, hg_w_in[slot], lbs[layer], hg_norm_g[slot], hg_w_out[slot])
        else:
            m = mla_mixer(h, positions, mla_w_in[slot], mla_q_norm_g[slot], mla_w_q_b[slot],
                          mla_kv_norm_g[slot], mla_w_kv_b[slot], mla_w_out[slot])
        h = layer_norm(DEEPNORM_ALPHA * h + m, ln_mix_g[layer], ln_mix_b[layer])
        f = swiglu(h, ffn_w_in[layer], ffn_w_out[layer])
        h = layer_norm(DEEPNORM_ALPHA * h + f, ln_ffn_g[layer], ln_ffn_b[layer])
    return h
```

```python
import functools
import math

import jax
import jax.numpy as jnp
from jax import lax
from jax.experimental import pallas as pl
from jax.experimental.pallas import tpu as pltpu

F32 = jnp.float32
BF16 = jnp.bfloat16

DEPTH = 4
N_MIXERS = 3
S5_GROUP = 16
S5_STATE = 64
S5_CHUNK = 64
HG_HEAD_DIM = 128
HG_CHUNK = 128
HG_BLOCK = 512
MLA_HEADS = 8
MLA_NOPE = 128
MLA_ROPE = 64
MLA_V = 128
MLA_Q_LORA = 384
MLA_KV_LORA = 256
ROPE_THETA = 10000.0
DEEPNORM_ALPHA = (2 * DEPTH) ** 0.25
LN_EPS = 1e-5
RMS_EPS = 1e-6
FFN_CHUNK = 512
VMEM_LIMIT = 56 * 1024 * 1024


def _params(*sem):
    return pltpu.CompilerParams(dimension_semantics=sem, vmem_limit_bytes=VMEM_LIMIT)


def _dot(a, b):
    return jnp.dot(a, b, preferred_element_type=F32)


def _dot_nt(a, b):
    return lax.dot_general(a, b, (((1,), (1,)), ((), ())), preferred_element_type=F32)


def _dot_tn(a, b):
    return lax.dot_general(a, b, (((0,), (0,)), ((), ())), preferred_element_type=F32)


def _layer_norm(y, g, b):
    mu = jnp.mean(y, axis=-1, keepdims=True)
    d = y - mu
    var = jnp.mean(d * d, axis=-1, keepdims=True)
    return d * lax.rsqrt(var + LN_EPS) * g + b


def _const_spec(shape):
    return pl.BlockSpec(shape, lambda *_: (0,) * len(shape))


def _matmul_kernel(x_ref, w_ref, o_ref):
    o_ref[...] = _dot(x_ref[...].astype(BF16), w_ref[...]).astype(o_ref.dtype)


def _matmul(x, w, tm, out_dtype=F32):
    t, k = x.shape
    n = w.shape[1]
    return pl.pallas_call(
        _matmul_kernel,
        grid=(t // tm,),
        in_specs=[pl.BlockSpec((tm, k), lambda i: (i, 0)), _const_spec((k, n))],
        out_specs=pl.BlockSpec((tm, n), lambda i: (i, 0)),
        out_shape=jax.ShapeDtypeStruct((t, n), out_dtype),
        compiler_params=_params("parallel"),
        name="proj",
    )(x, w)


def _ffn_kernel(x_ref, win_ref, wout_ref, g_ref, b_ref, o_ref, *, hid):
    x = x_ref[...]
    xb = x.astype(BF16)
    acc = DEEPNORM_ALPHA * x
    for lo in range(0, hid, FFN_CHUNK):
        w = min(FFN_CHUNK, hid - lo)
        gate = _dot(xb, win_ref[:, lo:lo + w])
        up = _dot(xb, win_ref[:, hid + lo:hid + lo + w])
        mid = (gate * jax.nn.sigmoid(gate) * up).astype(BF16)
        acc = acc + _dot(mid, wout_ref[lo:lo + w, :])
    o_ref[...] = _layer_norm(acc, g_ref[...], b_ref[...])


def _ffn_ln(x, w_in, w_out, g, b, tm=512):
    t, d = x.shape
    hid = w_out.shape[0]
    return pl.pallas_call(
        functools.partial(_ffn_kernel, hid=hid),
        grid=(t // tm,),
        in_specs=[pl.BlockSpec((tm, d), lambda i: (i, 0)),
                  pl.BlockSpec((d, 2 * hid), lambda i: (0, 0), pipeline_mode=pl.Buffered(1)),
                  pl.BlockSpec((hid, d), lambda i: (0, 0), pipeline_mode=pl.Buffered(1)),
                  _const_spec((1, d)), _const_spec((1, d))],
        out_specs=pl.BlockSpec((tm, d), lambda i: (i, 0)),
        out_shape=jax.ShapeDtypeStruct((t, d), F32),
        compiler_params=_params("parallel"),
        name="ffn_ln",
    )(x, w_in.astype(BF16), w_out.astype(BF16), g.reshape(1, d), b.reshape(1, d))


def _s5_tables(lam_re, lam_im, log_step, b_re, b_im, c_re, c_im, nlev):
    L = S5_CHUNK
    hi = lax.Precision.HIGHEST
    lam = lax.complex(lam_re.astype(F32), lam_im.astype(F32))
    step = jnp.exp(log_step.astype(F32))[..., None]
    lam_dt = lam * step
    lam_bar = jnp.exp(lam_dt)
    b_bar = ((lam_bar - 1.0) / lam)[..., None] * lax.complex(b_re.astype(F32), b_im.astype(F32))
    c = lax.complex(c_re.astype(F32), c_im.astype(F32))
    g, p, h = b_bar.shape[1], b_bar.shape[2], b_bar.shape[3]
    taus = jnp.arange(L + 1, dtype=F32)
    pw = jnp.exp(lam_dt[:, None] * taus[None, :, None, None])
    w = pw[:, :L, :, :, None] * b_bar[:, None]
    wr, wi = jnp.real(w), jnp.imag(w)
    cr, ci = jnp.real(c), jnp.imag(c)
    kern = (jnp.einsum('dgop,dtgpk->dtgok', cr, wr, precision=hi)
            - jnp.einsum('dgop,dtgpk->dtgok', ci, wi, precision=hi))
    kf, kb = kern[0], kern[1]
    kc = jnp.concatenate([kb[:0:-1], kf[:1] + kb[:1], kf[1:]], axis=0)
    strip = kc.transpose(1, 3, 0, 2).reshape(g, h, (2 * L - 1) * h)
    toep = jnp.stack([strip[:, :, (L - 1 - i) * h:(L - 1 - i) * h + L * h] for i in range(L)], axis=1)
    toep = toep.reshape(g, L * h, L * h).astype(BF16)

    def estate(wd):
        m = wd.transpose(1, 0, 3, 2).reshape(g, L * h, p)
        return jnp.concatenate([jnp.real(m), jnp.imag(m)], axis=-1)
    e_tab = jnp.concatenate([estate(w[0, ::-1]), estate(w[1])], axis=-1).astype(BF16)

    def gout(cd, pwd):
        m = (cd[None] * pwd[:, :, None, :]).transpose(1, 3, 0, 2).reshape(g, p, L * h)
        return jnp.concatenate([jnp.real(m), -jnp.imag(m)], axis=1)
    g_tab = jnp.concatenate([gout(c[0], pw[0, 1:L + 1]), gout(c[1], pw[1, L:0:-1])], axis=1).astype(BF16)

    k2 = (L * 2.0 ** jnp.arange(nlev, dtype=F32))
    a = jnp.exp(lam_dt[:, None] * k2[None, :, None, None])
    ar, ai = jnp.real(a), jnp.imag(a)
    p1 = jnp.concatenate([ar, ar], axis=-1).transpose(2, 1, 0, 3).reshape(g, nlev, 4 * p)
    p2 = jnp.concatenate([-ai, ai], axis=-1).transpose(2, 1, 0, 3).reshape(g, nlev, 4 * p)
    return toep, e_tab, g_tab, p1, p2


def _s5_core_kernel(u_ref, t_ref, e_ref, g_ref, p1_ref, p2_ref, y_ref, *, nc, nlev):
    u = u_ref[0]
    n = u.shape[0]
    two_p = 2 * S5_STATE
    y = _dot(u, t_ref[0])
    s = _dot(u, e_ref[0])
    cidx = lax.broadcasted_iota(jnp.int32, (n, two_p), 0) & (nc - 1)

    def cmul(x, k, d):
        a1 = p1_ref[0, k:k + 1, d * two_p:(d + 1) * two_p]
        a2 = p2_ref[0, k:k + 1, d * two_p:(d + 1) * two_p]
        return x * a1 + pltpu.roll(x, S5_STATE, 1) * a2

    xf, xb = s[:, :two_p], s[:, two_p:]
    for k in range(nlev):
        sh = 1 << k
        xf = xf + jnp.where(cidx >= sh, cmul(pltpu.roll(xf, sh, 0), k, 0), 0.0)
        xb = xb + jnp.where(cidx < nc - sh, cmul(pltpu.roll(xb, n - sh, 0), k, 1), 0.0)
    hf = jnp.where(cidx >= 1, pltpu.roll(xf, 1, 0), 0.0)
    hb = jnp.where(cidx < nc - 1, pltpu.roll(xb, n - 1, 0), 0.0)
    hcat = jnp.concatenate([hf, hb], axis=1).astype(BF16)
    y_ref[0] = y + _dot(hcat, g_ref[0])


def _s5_core(ub, tables, nc, nlev):
    toep, e_tab, g_tab, p1, p2 = tables
    g, n, lh = ub.shape
    blk = lambda a: pl.BlockSpec((1,) + a.shape[1:], lambda i: (i,) + (0,) * (a.ndim - 1))
    return pl.pallas_call(
        functools.partial(_s5_core_kernel, nc=nc, nlev=nlev),
        grid=(g,),
        in_specs=[blk(ub), blk(toep), blk(e_tab), blk(g_tab), blk(p1), blk(p2)],
        out_specs=pl.BlockSpec((1, n, lh), lambda i: (i, 0, 0)),
        out_shape=jax.ShapeDtypeStruct((g, n, lh), F32),
        compiler_params=_params("parallel"),
        name="s5_core",
    )(ub, toep, e_tab, g_tab, p1, p2)


def _s5_post_kernel(y_ref, u_ref, h_ref, d_ref, wg_ref, bg_ref, wo_ref, g_ref, b_ref, o_ref):
    y = y_ref[...] + d_ref[...] * u_ref[...]
    y = jax.nn.gelu(y)
    z = _dot(y.astype(BF16), wg_ref[...]) + bg_ref[...]
    y = y * jax.nn.sigmoid(z)
    m = _dot(y.astype(BF16), wo_ref[...])
    o_ref[...] = _layer_norm(DEEPNORM_ALPHA * h_ref[...] + m, g_ref[...], b_ref[...])


def _s5_mixer_ln(h, bsz, seq, w_in, lam_re, lam_im, log_step, b_re, b_im, c_re, c_im,
                 d_skip, w_glu, b_glu, w_out, ln_g, ln_b, tm=512):
    t, d = h.shape
    L = S5_CHUNK
    nc = seq // L
    nlev = max(1, (nc - 1).bit_length())
    assert nc & (nc - 1) == 0
    g = d // S5_GROUP
    u = _matmul(h, w_in.astype(BF16), tm)
    ub = u.astype(BF16).reshape(bsz * nc, L, g, S5_GROUP).transpose(2, 0, 1, 3).reshape(g, bsz * nc, L * S5_GROUP)
    tables = _s5_tables(lam_re, lam_im, log_step, b_re, b_im, c_re, c_im, nlev)
    y = _s5_core(ub, tables, nc, nlev)
    y = y.reshape(g, bsz * nc, L, S5_GROUP).transpose(1, 2, 0, 3).reshape(t, d)
    row = lambda v: v.reshape(1, d)
    tile = pl.BlockSpec((tm, d), lambda i: (i, 0))
    return pl.pallas_call(
        _s5_post_kernel,
        grid=(t // tm,),
        in_specs=[tile, tile, tile, _const_spec((1, d)), _const_spec((d, d)), _const_spec((1, d)),
                  _const_spec((d, d)), _const_spec((1, d)), _const_spec((1, d))],
        out_specs=tile,
        out_shape=jax.ShapeDtypeStruct((t, d), F32),
        compiler_params=_params("parallel"),
        name="s5_post",
    )(y, u, h, row(d_skip), w_glu.astype(BF16), row(b_glu), w_out.astype(BF16), row(ln_g), row(ln_b))


def _hg_masks(c):
    nlev = c.bit_length() - 1
    t = jnp.arange(c)[:, None]
    s = jnp.arange(c)[None, :]
    ms = []
    for lev in range(nlev):
        w = 1 << lev
        ms.append(((t & w) != 0) & ((s & w) == 0) & ((t >> (lev + 1)) == (s >> (lev + 1))))
    ms.append(t == s)
    fw = jnp.stack(ms).astype(F32)
    return jnp.stack([fw, fw.transpose(0, 2, 1)])


def _hg_gate(z, la, l1m, onem):
    ls = jnp.minimum(z, 0.0) - jnp.log1p(jnp.exp(-jnp.abs(z)))
    b = l1m + ls
    lf = jnp.maximum(la, b) + jnp.log1p(jnp.exp(-jnp.abs(la - b)))
    return lf, onem * jax.nn.sigmoid(-z)


def _hg_chunk(q, z, v, la, l1m, onem, mask_ref, d, st_ref):
    c = q.shape[0]
    nlev = c.bit_length() - 1
    q = q * jax.nn.sigmoid(q)
    lf, k = _hg_gate(z, la, l1m, onem)
    qb, kb = q.astype(BF16), k.astype(BF16)
    row = lax.broadcasted_iota(jnp.int32, lf.shape, 0)
    scores = mask_ref[d, nlev] * _dot_nt(qb, kb)
    p, tot = lf, lf
    for lev in range(nlev):
        w = 1 << lev
        odd = (row & w) != 0
        if d == 0:
            e = jnp.where(odd, p, tot - p)
        else:
            e = jnp.where(odd, p - lf, tot - p + lf)
        f = jnp.exp(e)
        scores = scores + mask_ref[d, lev] * _dot_nt((q * f).astype(BF16), (k * f).astype(BF16))
        up = pltpu.roll(tot, w, 0)
        dn = pltpu.roll(tot, c - w, 0)
        p = p + jnp.where(odd, up, 0.0)
        tot = tot + jnp.where(odd, up, dn)
    if d == 0:
        eq, ek = p, tot - p
    else:
        eq, ek = tot - p + lf, p - lf
    st = st_ref[...]
    o = _dot(scores.astype(BF16), v.astype(BF16))
    o = o + _dot_nt((q * jnp.exp(eq)).astype(BF16), st.astype(BF16))
    st_ref[...] = st * jnp.exp(tot[0:1, :]) + _dot_tn(v.astype(BF16), (k * jnp.exp(ek)).astype(BF16))
    return o


def _hg_rec_kernel(qf_ref, zf_ref, vf_ref, qb_ref, zb_ref, vb_ref, la_ref, l1m_ref, onem_ref, mask_ref,
                   of_ref, ob_ref, stf_ref, stb_ref, *, nsub):
    @pl.when(pl.program_id(2) == 0)
    def _():
        stf_ref[...] = jnp.zeros_like(stf_ref)
        stb_ref[...] = jnp.zeros_like(stb_ref)

    la, l1m, onem = la_ref[...], l1m_ref[...], onem_ref[...]
    c = HG_CHUNK

    def body(j, carry):
        lo = pl.multiple_of(j * c, c)
        hi = pl.multiple_of((nsub - 1 - j) * c, c)
        rf = pl.ds(lo, c)
        rb = pl.ds(hi, c)
        of_ref[rf, :] = _hg_chunk(qf_ref[rf, :], zf_ref[rf, :], vf_ref[rf, :], la, l1m, onem, mask_ref, 0, stf_ref)
        ob_ref[rb, :] = _hg_chunk(qb_ref[rb, :], zb_ref[rb, :], vb_ref[rb, :], la, l1m, onem, mask_ref, 1, stb_ref)
        return carry

    lax.fori_loop(0, nsub, body, 0)


def _hg_post_kernel(of_ref, ob_ref, g_ref, h_ref, ng_ref, wo_ref, lg_ref, lb_ref, o_ref):
    o = of_ref[...] + ob_ref[...]
    hd = HG_HEAD_DIM
    parts = []
    for i in range(o.shape[1] // hd):
        oh = o[:, i * hd:(i + 1) * hd]
        parts.append(oh * lax.rsqrt(jnp.mean(oh * oh, axis=-1, keepdims=True) + RMS_EPS))
    on = jnp.concatenate(parts, axis=1) * ng_ref[...]
    gate = g_ref[...]
    y = on * (gate * jax.nn.sigmoid(gate))
    m = _dot(y.astype(BF16), wo_ref[...])
    o_ref[...] = _layer_norm(DEEPNORM_ALPHA * h_ref[...] + m, lg_ref[...], lb_ref[...])


def _hg_mixer_ln(h, bsz, seq, w_in, lb, norm_g, w_out, ln_g, ln_b, tm=512):
    t, d = h.shape
    hd = HG_HEAD_DIM
    nh = d // hd
    blk = min(HG_BLOCK, seq)
    nblk = seq // blk
    proj = _matmul(h, w_in.astype(BF16), tm)
    lb = lb.astype(F32).reshape(1, d)
    la, l1m, onem = jnp.log(lb), jnp.log1p(-lb), 1.0 - lb
    masks = _hg_masks(HG_CHUNK)

    def fw(col):
        return pl.BlockSpec((blk, hd), lambda b, hh, i: (b * nblk + i, col * nh + hh))

    def bw(col):
        return pl.BlockSpec((blk, hd), lambda b, hh, i: (b * nblk + nblk - 1 - i, col * nh + hh))

    vec = pl.BlockSpec((1, hd), lambda b, hh, i: (0, hh))
    o_fw, o_bw = pl.pallas_call(
        functools.partial(_hg_rec_kernel, nsub=blk // HG_CHUNK),
        grid=(bsz, nh, nblk),
        in_specs=[fw(0), fw(1), fw(3), bw(0), bw(2), bw(3), vec, vec, vec, _const_spec(masks.shape)],
        out_specs=[pl.BlockSpec((blk, hd), lambda b, hh, i: (b * nblk + i, hh)),
                   pl.BlockSpec((blk, hd), lambda b, hh, i: (b * nblk + nblk - 1 - i, hh))],
        out_shape=[jax.ShapeDtypeStruct((t, d), F32)] * 2,
        scratch_shapes=[pltpu.VMEM((hd, hd), F32), pltpu.VMEM((hd, hd), F32)],
        compiler_params=_params("parallel", "parallel", "arbitrary"),
        name="hg_rec",
    )(proj, proj, proj, proj, proj, proj, la, l1m, onem, masks)

    row = lambda v: v.reshape(1, d)
    tile = pl.BlockSpec((tm, d), lambda i: (i, 0))
    return pl.pallas_call(
        _hg_post_kernel,
        grid=(t // tm,),
        in_specs=[tile, tile, pl.BlockSpec((tm, d), lambda i: (i, 4)), tile, _const_spec((1, d)),
                  _const_spec((d, d)), _const_spec((1, d)), _const_spec((1, d))],
        out_specs=tile,
        out_shape=jax.ShapeDtypeStruct((t, d), F32),
        compiler_params=_params("parallel"),
        name="hg_post",
    )(o_fw, o_bw, proj, h, jnp.tile(norm_g.astype(F32), nh).reshape(1, d), w_out.astype(BF16), row(ln_g), row(ln_b))


def _mla_proj_kernel(x_ref, pos_ref, winT_ref, gq_ref, gkv_ref, wqT_ref, wkvT_ref, invf_ref,
                     qT_ref, kT_ref, vT_ref):
    ql, kvl, half = MLA_Q_LORA, MLA_KV_LORA, MLA_ROPE // 2
    dq = MLA_NOPE + MLA_ROPE
    lat = _dot_nt(winT_ref[...], x_ref[...].astype(BF16))
    q_lat, kv_lat, k_rope = lat[:ql], lat[ql:ql + kvl], lat[ql + kvl:]

    def rms(v, g):
        return (v * lax.rsqrt(jnp.mean(v * v, axis=0, keepdims=True) + RMS_EPS) * g).astype(BF16)

    q = _dot(wqT_ref[...], rms(q_lat, gq_ref[...]))
    kv = _dot(wkvT_ref[...], rms(kv_lat, gkv_ref[...]))
    ang = invf_ref[...] * pos_ref[0].astype(F32)
    cos, sin = jnp.cos(ang), jnp.sin(ang)

    def rope(t1, t2):
        return t1 * cos - t2 * sin, t1 * sin + t2 * cos

    scale = dq ** -0.5
    k1, k2 = rope(k_rope[:half], k_rope[half:])
    k1, k2 = k1.astype(BF16), k2.astype(BF16)
    for hh in range(MLA_HEADS):
        qh = q[hh * dq:(hh + 1) * dq]
        q1, q2 = rope(qh[MLA_NOPE:MLA_NOPE + half], qh[MLA_NOPE + half:])
        qT_ref[0, hh, 0:MLA_NOPE, :] = (qh[:MLA_NOPE] * scale).astype(BF16)
        qT_ref[0, hh, MLA_NOPE:MLA_NOPE + half, :] = (q1 * scale).astype(BF16)
        qT_ref[0, hh, MLA_NOPE + half:dq, :] = (q2 * scale).astype(BF16)
        kvh = kv[hh * (MLA_NOPE + MLA_V):(hh + 1) * (MLA_NOPE + MLA_V)]
        kT_ref[0, hh, 0:MLA_NOPE, :] = kvh[:MLA_NOPE].astype(BF16)
        kT_ref[0, hh, MLA_NOPE:MLA_NOPE + half, :] = k1
        kT_ref[0, hh, MLA_NOPE + half:dq, :] = k2
        vT_ref[0, hh] = kvh[MLA_NOPE:].astype(BF16)


def _attn_kernel(qT_ref, k_ref, vT_ref, o_ref, m_sc, l_sc, acc_sc, *, nkv):
    qT = qT_ref[0, 0]
    m_sc[...] = jnp.full_like(m_sc, -jnp.inf)
    l_sc[...] = jnp.zeros_like(l_sc)
    acc_sc[...] = jnp.zeros_like(acc_sc)

    def body(j, carry):
        s = _dot(k_ref[0, 0, j], qT)
        m_prev = m_sc[...]
        m_new = jnp.maximum(m_prev, jnp.max(s, axis=0, keepdims=True))
        alpha = jnp.exp(m_prev - m_new)
        p = jnp.exp(s - m_new)
        l_sc[...] = alpha * l_sc[...] + jnp.sum(p, axis=0, keepdims=True)
        acc_sc[...] = alpha * acc_sc[...] + _dot(vT_ref[0, 0, j], p.astype(BF16))
        m_sc[...] = m_new
        return carry

    lax.fori_loop(0, nkv, body, 0)
    o_ref[0, 0] = (acc_sc[...] / l_sc[...]).astype(o_ref.dtype)


def _mla_out_kernel(oT_ref, h_ref, wo_ref, g_ref, b_ref, o_ref):
    oT = oT_ref[0]
    oT = oT.reshape(oT.shape[0] * oT.shape[1], oT.shape[2])
    m = _dot_tn(oT, wo_ref[...])
    o_ref[...] = _layer_norm(DEEPNORM_ALPHA * h_ref[...] + m, g_ref[...], b_ref[...])


def _mla_mixer_ln(h, positions, bsz, seq, w_in, q_norm_g, w_q_b, kv_norm_g, w_kv_b, w_out, ln_g, ln_b,
                  tm=512, tq=512, tk=512):
    t, d = h.shape
    nh, dq, dv = MLA_HEADS, MLA_NOPE + MLA_ROPE, MLA_V
    half = MLA_ROPE // 2
    inv_freq = 1.0 / (ROPE_THETA ** (jnp.arange(half, dtype=F32) * (2.0 / MLA_ROPE)))
    nlat = w_in.shape[1]
    nt = seq // tm
    head = lambda n: pl.BlockSpec((1, nh, n, tm), lambda b, i: (b, 0, 0, i))
    qT, kT, vT = pl.pallas_call(
        _mla_proj_kernel,
        grid=(bsz, nt),
        in_specs=[pl.BlockSpec((tm, d), lambda b, i: (b * nt + i, 0)),
                  pl.BlockSpec((1, 1, tm), lambda b, i: (b, 0, i)),
                  _const_spec((nlat, d)), _const_spec((MLA_Q_LORA, 1)), _const_spec((MLA_KV_LORA, 1)),
                  _const_spec((nh * dq, MLA_Q_LORA)), _const_spec((nh * (MLA_NOPE + dv), MLA_KV_LORA)),
                  _const_spec((half, 1))],
        out_specs=[head(dq), head(dq), head(dv)],
        out_shape=[jax.ShapeDtypeStruct((bsz, nh, dq, seq), BF16),
                   jax.ShapeDtypeStruct((bsz, nh, dq, seq), BF16),
                   jax.ShapeDtypeStruct((bsz, nh, dv, seq), BF16)],
        compiler_params=_params("parallel", "parallel"),
        name="mla_proj",
    )(h, positions.reshape(bsz, 1, seq), w_in.T.astype(BF16), q_norm_g.astype(F32).reshape(-1, 1),
      kv_norm_g.astype(F32).reshape(-1, 1), w_q_b.T.astype(BF16), w_kv_b.T.astype(BF16),
      inv_freq.reshape(half, 1))

    nkv = seq // tk
    kc = jnp.swapaxes(kT, 2, 3).reshape(bsz, nh, nkv, tk, dq)
    vc = vT.reshape(bsz, nh, dv, nkv, tk).transpose(0, 1, 3, 2, 4)
    oT = pl.pallas_call(
        functools.partial(_attn_kernel, nkv=nkv),
        grid=(bsz, nh, seq // tq),
        in_specs=[pl.BlockSpec((1, 1, dq, tq), lambda b, hh, i: (b, hh, 0, i)),
                  pl.BlockSpec((1, 1, nkv, tk, dq), lambda b, hh, i: (b, hh, 0, 0, 0)),
                  pl.BlockSpec((1, 1, nkv, dv, tk), lambda b, hh, i: (b, hh, 0, 0, 0))],
        out_specs=pl.BlockSpec((1, 1, dv, tq), lambda b, hh, i: (b, hh, 0, i)),
        out_shape=jax.ShapeDtypeStruct((bsz, nh, dv, seq), BF16),
        scratch_shapes=[pltpu.VMEM((1, tq), F32), pltpu.VMEM((1, tq), F32), pltpu.VMEM((dv, tq), F32)],
        compiler_params=_params("parallel", "parallel", "arbitrary"),
        name="mla_attn",
    )(qT, kc, vc)

    return pl.pallas_call(
        _mla_out_kernel,
        grid=(bsz, nt),
        in_specs=[pl.BlockSpec((1, nh, dv, tm), lambda b, i: (b, 0, 0, i)),
                  pl.BlockSpec((tm, d), lambda b, i: (b * nt + i, 0)),
                  _const_spec((nh * dv, d)), _const_spec((1, d)), _const_spec((1, d))],
        out_specs=pl.BlockSpec((tm, d), lambda b, i: (b * nt + i, 0)),
        out_shape=jax.ShapeDtypeStruct((t, d), F32),
        compiler_params=_params("parallel", "parallel"),
        name="mla_out",
    )(oT, h, w_out.astype(BF16), ln_g.reshape(1, d), ln_b.reshape(1, d))


def kernel(x, positions, ln_mix_g, ln_mix_b, ln_ffn_g, ln_ffn_b, ffn_w_in, ffn_w_out, s5_w_in, s5_lam_re, s5_lam_im, s5_log_step, s5_b_re, s5_b_im, s5_c_re, s5_c_im, s5_d, s5_w_glu, s5_b_glu, s5_w_out, hg_w_in, hg_lower_bound, hg_norm_g, hg_w_out, mla_w_in, mla_q_norm_g, mla_w_q_b, mla_kv_norm_g, mla_w_kv_b, mla_w_out):
    bsz, seq, d = x.shape
    lbs = jax.nn.softmax(hg_lower_bound.astype(F32), axis=0)
    lbs = jnp.cumsum(lbs, axis=0) - lbs[0]
    h = x.reshape(bsz * seq, d)
    for layer in range(DEPTH):
        kind = layer % N_MIXERS
        slot = layer // N_MIXERS
        lg, lb = ln_mix_g[layer], ln_mix_b[layer]
        if kind == 0:
            h = _s5_mixer_ln(h, bsz, seq, s5_w_in[slot], s5_lam_re[slot], s5_lam_im[slot], s5_log_step[slot],
                             s5_b_re[slot], s5_b_im[slot], s5_c_re[slot], s5_c_im[slot], s5_d[slot],
                             s5_w_glu[slot], s5_b_glu[slot], s5_w_out[slot], lg, lb)
        elif kind == 1:
            h = _hg_mixer_ln(h, bsz, seq, hg_w_in[slot], lbs[layer], hg_norm_g[slot], hg_w_out[slot], lg, lb)
        else:
            h = _mla_mixer_ln(h, positions, bsz, seq, mla_w_in[slot], mla_q_norm_g[slot], mla_w_q_b[slot],
                              mla_kv_norm_g[slot], mla_w_kv_b[slot], mla_w_out[slot], lg, lb)
        h = _ffn_ln(h, ffn_w_in[layer], ffn_w_out[layer], ln_ffn_g[layer], ln_ffn_b[layer])
    return h.reshape(bsz, seq, d)
```

```python
import functools
import math

import jax
import jax.numpy as jnp
from jax import lax
from jax.experimental import pallas as pl
from jax.experimental.pallas import tpu as pltpu

F32 = jnp.float32
BF16 = jnp.bfloat16

DEPTH = 4
N_MIXERS = 3
S5_GROUP = 16
S5_STATE = 64
S5_CHUNK = 64
HG_HEAD_DIM = 128
HG_CHUNK = 128
HG_BLOCK = 512
MLA_HEADS = 8
MLA_NOPE = 128
MLA_ROPE = 64
MLA_V = 128
MLA_Q_LORA = 384
MLA_KV_LORA = 256
ROPE_THETA = 10000.0
DEEPNORM_ALPHA = (2 * DEPTH) ** 0.25
LN_EPS = 1e-5
RMS_EPS = 1e-6
FFN_CHUNK = 512
VMEM_LIMIT = 56 * 1024 * 1024


def _params(*sem):
    return pltpu.CompilerParams(dimension_semantics=sem, vmem_limit_bytes=VMEM_LIMIT)


def _dot(a, b):
    return jnp.dot(a, b, preferred_element_type=F32)


def _dot_nt(a, b):
    return lax.dot_general(a, b, (((1,), (1,)), ((), ())), preferred_element_type=F32)


def _dot_tn(a, b):
    return lax.dot_general(a, b, (((0,), (0,)), ((), ())), preferred_element_type=F32)


def _layer_norm(y, g, b):
    mu = jnp.mean(y, axis=-1, keepdims=True)
    d = y - mu
    var = jnp.mean(d * d, axis=-1, keepdims=True)
    return d * lax.rsqrt(var + LN_EPS) * g + b


def _const_spec(shape):
    return pl.BlockSpec(shape, lambda *_: (0,) * len(shape))


def _matmul_kernel(x_ref, w_ref, o_ref):
    o_ref[...] = _dot(x_ref[...].astype(BF16), w_ref[...]).astype(o_ref.dtype)


def _matmul(x, w, tm, out_dtype=F32):
    t, k = x.shape
    n = w.shape[1]
    return pl.pallas_call(
        _matmul_kernel,
        grid=(t // tm,),
        in_specs=[pl.BlockSpec((tm, k), lambda i: (i, 0)), _const_spec((k, n))],
        out_specs=pl.BlockSpec((tm, n), lambda i: (i, 0)),
        out_shape=jax.ShapeDtypeStruct((t, n), out_dtype),
        compiler_params=_params("parallel"),
        name="proj",
    )(x, w)


def _ffn_kernel(x_ref, win_ref, wout_ref, g_ref, b_ref, o_ref, *, hid):
    x = x_ref[...]
    xb = x.astype(BF16)
    acc = DEEPNORM_ALPHA * x
    for lo in range(0, hid, FFN_CHUNK):
        w = min(FFN_CHUNK, hid - lo)
        gate = _dot(xb, win_ref[:, lo:lo + w])
        up = _dot(xb, win_ref[:, hid + lo:hid + lo + w])
        mid = (gate * jax.nn.sigmoid(gate) * up).astype(BF16)
        acc = acc + _dot(mid, wout_ref[lo:lo + w, :])
    o_ref[...] = _layer_norm(acc, g_ref[...], b_ref[...])


def _ffn_ln(x, w_in, w_out, g, b, tm=512):
    t, d = x.shape
    hid = w_out.shape[0]
    return pl.pallas_call(
        functools.partial(_ffn_kernel, hid=hid),
        grid=(t // tm,),
        in_specs=[pl.BlockSpec((tm, d), lambda i: (i, 0)),
                  pl.BlockSpec((d, 2 * hid), lambda i: (0, 0), pipeline_mode=pl.Buffered(1)),
                  pl.BlockSpec((hid, d), lambda i: (0, 0), pipeline_mode=pl.Buffered(1)),
                  _const_spec((1, d)), _const_spec((1, d))],
        out_specs=pl.BlockSpec((tm, d), lambda i: (i, 0)),
        out_shape=jax.ShapeDtypeStruct((t, d), F32),
        compiler_params=_params("parallel"),
        name="ffn_ln",
    )(x, w_in.astype(BF16), w_out.astype(BF16), g.reshape(1, d), b.reshape(1, d))


def _s5_tables(lam_re, lam_im, log_step, b_re, b_im, c_re, c_im, nlev):
    L = S5_CHUNK
    hi = lax.Precision.HIGHEST
    lam = lax.complex(lam_re.astype(F32), lam_im.astype(F32))
    step = jnp.exp(log_step.astype(F32))[..., None]
    lam_dt = lam * step
    lam_bar = jnp.exp(lam_dt)
    b_bar = ((lam_bar - 1.0) / lam)[..., None] * lax.complex(b_re.astype(F32), b_im.astype(F32))
    c = lax.complex(c_re.astype(F32), c_im.astype(F32))
    g, p, h = b_bar.shape[1], b_bar.shape[2], b_bar.shape[3]
    taus = jnp.arange(L + 1, dtype=F32)
    pw = jnp.exp(lam_dt[:, None] * taus[None, :, None, None])
    w = pw[:, :L, :, :, None] * b_bar[:, None]
    wr, wi = jnp.real(w), jnp.imag(w)
    cr, ci = jnp.real(c), jnp.imag(c)
    kern = (jnp.einsum('dgop,dtgpk->dtgok', cr, wr, precision=hi)
            - jnp.einsum('dgop,dtgpk->dtgok', ci, wi, precision=hi))
    kf, kb = kern[0], kern[1]
    kc = jnp.concatenate([kb[:0:-1], kf[:1] + kb[:1], kf[1:]], axis=0)
    strip = kc.transpose(1, 3, 0, 2).reshape(g, h, (2 * L - 1) * h)
    toep = jnp.stack([strip[:, :, (L - 1 - i) * h:(L - 1 - i) * h + L * h] for i in range(L)], axis=1)
    toep = toep.reshape(g, L * h, L * h).astype(BF16)

    def estate(wd):
        m = wd.transpose(1, 0, 3, 2).reshape(g, L * h, p)
        return jnp.concatenate([jnp.real(m), jnp.imag(m)], axis=-1)
    e_tab = jnp.concatenate([estate(w[0, ::-1]), estate(w[1])], axis=-1).astype(BF16)

    def gout(cd, pwd):
        m = (cd[None] * pwd[:, :, None, :]).transpose(1, 3, 0, 2).reshape(g, p, L * h)
        return jnp.concatenate([jnp.real(m), -jnp.imag(m)], axis=1)
    g_tab = jnp.concatenate([gout(c[0], pw[0, 1:L + 1]), gout(c[1], pw[1, L:0:-1])], axis=1).astype(BF16)

    k2 = (L * 2.0 ** jnp.arange(nlev, dtype=F32))
    a = jnp.exp(lam_dt[:, None] * k2[None, :, None, None])
    ar, ai = jnp.real(a), jnp.imag(a)
    p1 = jnp.concatenate([ar, ar], axis=-1).transpose(2, 1, 0, 3).reshape(g, nlev, 4 * p)
    p2 = jnp.concatenate([-ai, ai], axis=-1).transpose(2, 1, 0, 3).reshape(g, nlev, 4 * p)
    return toep, e_tab, g_tab, p1, p2


def _s5_core_kernel(u_ref, t_ref, e_ref, g_ref, p1_ref, p2_ref, y_ref, *, nc, nlev):
    u = u_ref[0]
    n = u.shape[0]
    two_p = 2 * S5_STATE
    y = _dot(u, t_ref[0])
    s = _dot(u, e_ref[0])
    cidx = lax.broadcasted_iota(jnp.int32, (n, two_p), 0) & (nc - 1)

    def cmul(x, k, d):
        a1 = p1_ref[0, k:k + 1, d * two_p:(d + 1) * two_p]
        a2 = p2_ref[0, k:k + 1, d * two_p:(d + 1) * two_p]
        return x * a1 + pltpu.roll(x, S5_STATE, 1) * a2

    xf, xb = s[:, :two_p], s[:, two_p:]
    for k in range(nlev):
        sh = 1 << k
        xf = xf + jnp.where(cidx >= sh, cmul(pltpu.roll(xf, sh, 0), k, 0), 0.0)
        xb = xb + jnp.where(cidx < nc - sh, cmul(pltpu.roll(xb, n - sh, 0), k, 1), 0.0)
    hf = jnp.where(cidx >= 1, pltpu.roll(xf, 1, 0), 0.0)
    hb = jnp.where(cidx < nc - 1, pltpu.roll(xb, n - 1, 0), 0.0)
    hcat = jnp.concatenate([hf, hb], axis=1).astype(BF16)
    y_ref[0] = y + _dot(hcat, g_ref[0])


def _s5_core(ub, tables, nc, nlev):
    toep, e_tab, g_tab, p1, p2 = tables
    g, n, lh = ub.shape
    blk = lambda a: pl.BlockSpec((1,) + a.shape[1:], lambda i: (i,) + (0,) * (a.ndim - 1))
    return pl.pallas_call(
        functools.partial(_s5_core_kernel, nc=nc, nlev=nlev),
        grid=(g,),
        in_specs=[blk(ub), blk(toep), blk(e_tab), blk(g_tab), blk(p1), blk(p2)],
        out_specs=pl.BlockSpec((1, n, lh), lambda i: (i, 0, 0)),
        out_shape=jax.ShapeDtypeStruct((g, n, lh), F32),
        compiler_params=_params("parallel"),
        name="s5_core",
    )(ub, toep, e_tab, g_tab, p1, p2)


def _s5_post_kernel(y_ref, u_ref, h_ref, d_ref, wg_ref, bg_ref, wo_ref, g_ref, b_ref, o_ref):
    y = y_ref[...] + d_ref[...] * u_ref[...]
    y = jax.nn.gelu(y)
    z = _dot(y.astype(BF16), wg_ref[...]) + bg_ref[...]
    y = y * jax.nn.sigmoid(z)
    m = _dot(y.astype(BF16), wo_ref[...])
    o_ref[...] = _layer_norm(DEEPNORM_ALPHA * h_ref[...] + m, g_ref[...], b_ref[...])


def _s5_mixer_ln(h, bsz, seq, w_in, lam_re, lam_im, log_step, b_re, b_im, c_re, c_im,
                 d_skip, w_glu, b_glu, w_out, ln_g, ln_b, tm=512):
    t, d = h.shape
    L = S5_CHUNK
    nc = seq // L
    nlev = max(1, (nc - 1).bit_length())
    assert nc & (nc - 1) == 0
    g = d // S5_GROUP
    u = _matmul(h, w_in.astype(BF16), tm)
    ub = u.astype(BF16).reshape(bsz * nc, L, g, S5_GROUP).transpose(2, 0, 1, 3).reshape(g, bsz * nc, L * S5_GROUP)
    tables = _s5_tables(lam_re, lam_im, log_step, b_re, b_im, c_re, c_im, nlev)
    y = _s5_core(ub, tables, nc, nlev)
    y = y.reshape(g, bsz * nc, L, S5_GROUP).transpose(1, 2, 0, 3).reshape(t, d)
    row = lambda v: v.reshape(1, d)
    tile = pl.BlockSpec((tm, d), lambda i: (i, 0))
    return pl.pallas_call(
        _s5_post_kernel,
        grid=(t // tm,),
        in_specs=[tile, tile, tile, _const_spec((1, d)), _const_spec((d, d)), _const_spec((1, d)),
                  _const_spec((d, d)), _const_spec((1, d)), _const_spec((1, d))],
        out_specs=tile,
        out_shape=jax.ShapeDtypeStruct((t, d), F32),
        compiler_params=_params("parallel"),
        name="s5_post",
    )(y, u, h, row(d_skip), w_glu.astype(BF16), row(b_glu), w_out.astype(BF16), row(ln_g), row(ln_b))


def _hg_masks(c):
    nlev = c.bit_length() - 1
    t = jnp.arange(c)[:, None]
    s = jnp.arange(c)[None, :]
    ms = []
    for lev in range(nlev):
        w = 1 << lev
        ms.append(((t & w) != 0) & ((s & w) == 0) & ((t >> (lev + 1)) == (s >> (lev + 1))))
    ms.append(t == s)
    fw = jnp.stack(ms).astype(F32)
    return jnp.stack([fw, fw.transpose(0, 2, 1)])


def _hg_gate(z, la, l1m, onem):
    ls = jnp.minimum(z, 0.0) - jnp.log1p(jnp.exp(-jnp.abs(z)))
    b = l1m + ls
    lf = jnp.maximum(la, b) + jnp.log1p(jnp.exp(-jnp.abs(la - b)))
    return lf, onem * jax.nn.sigmoid(-z)


def _hg_chunk(q, z, v, la, l1m, onem, mask_ref, d, st_ref):
    c = q.shape[0]
    nlev = c.bit_length() - 1
    q = q * jax.nn.sigmoid(q)
    lf, k = _hg_gate(z, la, l1m, onem)
    qb, kb = q.astype(BF16), k.astype(BF16)
    row = lax.broadcasted_iota(jnp.int32, lf.shape, 0)
    scores = mask_ref[d, nlev] * _dot_nt(qb, kb)
    p, tot = lf, lf
    for lev in range(nlev):
        w = 1 << lev
        odd = (row & w) != 0
        if d == 0:
            e = jnp.where(odd, p, tot - p)
        else:
            e = jnp.where(odd, p - lf, tot - p + lf)
        f = jnp.exp(e)
        scores = scores + mask_ref[d, lev] * _dot_nt((q * f).astype(BF16), (k * f).astype(BF16))
        up = pltpu.roll(tot, w, 0)
        dn = pltpu.roll(tot, c - w, 0)
        p = p + jnp.where(odd, up, 0.0)
        tot = tot + jnp.where(odd, up, dn)
    if d == 0:
        eq, ek = p, tot - p
    else:
        eq, ek = tot - p + lf, p - lf
    st = st_ref[...]
    o = _dot(scores.astype(BF16), v.astype(BF16))
    o = o + _dot_nt((q * jnp.exp(eq)).astype(BF16), st.astype(BF16))
    st_ref[...] = st * jnp.exp(tot[0:1, :]) + _dot_tn(v.astype(BF16), (k * jnp.exp(ek)).astype(BF16))
    return o


def _hg_rec_kernel(qf_ref, zf_ref, vf_ref, qb_ref, zb_ref, vb_ref, la_ref, l1m_ref, onem_ref, mask_ref,
                   of_ref, ob_ref, stf_ref, stb_ref, *, nsub):
    @pl.when(pl.program_id(2) == 0)
    def _():
        stf_ref[...] = jnp.zeros_like(stf_ref)
        stb_ref[...] = jnp.zeros_like(stb_ref)

    la, l1m, onem = la_ref[...], l1m_ref[...], onem_ref[...]
    c = HG_CHUNK

    def body(j, carry):
        lo = pl.multiple_of(j * c, c)
        hi = pl.multiple_of((nsub - 1 - j) * c, c)
        rf = pl.ds(lo, c)
        rb = pl.ds(hi, c)
        of_ref[rf, :] = _hg_chunk(qf_ref[rf, :], zf_ref[rf, :], vf_ref[rf, :], la, l1m, onem, mask_ref, 0, stf_ref)
        ob_ref[rb, :] = _hg_chunk(qb_ref[rb, :], zb_ref[rb, :], vb_ref[rb, :], la, l1m, onem, mask_ref, 1, stb_ref)
        return carry

    lax.fori_loop(0, nsub, body, 0)


def _hg_post_kernel(of_ref, ob_ref, g_ref, h_ref, ng_ref, wo_ref, lg_ref, lb_ref, o_ref):
    o = of_ref[...] + ob_ref[...]
    hd = HG_HEAD_DIM
    parts = []
    for i in range(o.shape[1] // hd):
        oh = o[:, i * hd:(i + 1) * hd]
        parts.append(oh * lax.rsqrt(jnp.mean(oh * oh, axis=-1, keepdims=True) + RMS_EPS))
    on = jnp.concatenate(parts, axis=1) * ng_ref[...]
    gate = g_ref[...]
    y = on * (gate * jax.nn.sigmoid(gate))
    m = _dot(y.astype(BF16), wo_ref[...])
    o_ref[...] = _layer_norm(DEEPNORM_ALPHA * h_ref[...] + m, lg_ref[...], lb_ref[...])


def _hg_mixer_ln(h, bsz, seq, w_in, lb, norm_g, w_out, ln_g, ln_b, tm=512):
    t, d = h.shape
    hd = HG_HEAD_DIM
    nh = d // hd
    blk = min(HG_BLOCK, seq)
    nblk = seq // blk
    proj = _matmul(h, w_in.astype(BF16), tm)
    lb = lb.astype(F32).reshape(1, d)
    la, l1m, onem = jnp.log(lb), jnp.log1p(-lb), 1.0 - lb
    masks = _hg_masks(HG_CHUNK)

    def fw(col):
        return pl.BlockSpec((blk, hd), lambda b, hh, i: (b * nblk + i, col * nh + hh))

    def bw(col):
        return pl.BlockSpec((blk, hd), lambda b, hh, i: (b * nblk + nblk - 1 - i, col * nh + hh))

    vec = pl.BlockSpec((1, hd), lambda b, hh, i: (0, hh))
    o_fw, o_bw = pl.pallas_call(
        functools.partial(_hg_rec_kernel, nsub=blk // HG_CHUNK),
        grid=(bsz, nh, nblk),
        in_specs=[fw(0), fw(1), fw(3), bw(0), bw(2), bw(3), vec, vec, vec, _const_spec(masks.shape)],
        out_specs=[pl.BlockSpec((blk, hd), lambda b, hh, i: (b * nblk + i, hh)),
                   pl.BlockSpec((blk, hd), lambda b, hh, i: (b * nblk + nblk - 1 - i, hh))],
        out_shape=[jax.ShapeDtypeStruct((t, d), F32)] * 2,
        scratch_shapes=[pltpu.VMEM((hd, hd), F32), pltpu.VMEM((hd, hd), F32)],
        compiler_params=_params("parallel", "parallel", "arbitrary"),
        name="hg_rec",
    )(proj, proj, proj, proj, proj, proj, la, l1m, onem, masks)

    row = lambda v: v.reshape(1, d)
    tile = pl.BlockSpec((tm, d), lambda i: (i, 0))
    return pl.pallas_call(
        _hg_post_kernel,
        grid=(t // tm,),
        in_specs=[tile, tile, pl.BlockSpec((tm, d), lambda i: (i, 4)), tile, _const_spec((1, d)),
                  _const_spec((d, d)), _const_spec((1, d)), _const_spec((1, d))],
        out_specs=tile,
        out_shape=jax.ShapeDtypeStruct((t, d), F32),
        compiler_params=_params("parallel"),
        name="hg_post",
    )(o_fw, o_bw, proj, h, jnp.tile(norm_g.astype(F32), nh).reshape(1, d), w_out.astype(BF16), row(ln_g), row(ln_b))


def _mla_proj_kernel(x_ref, pos_ref, winT_ref, gq_ref, gkv_ref, wqT_ref, wkvT_ref, invf_ref,
                     qT_ref, kT_ref, vT_ref):
    ql, kvl, half = MLA_Q_LORA, MLA_KV_LORA, MLA_ROPE // 2
    dq = MLA_NOPE + MLA_ROPE
    lat = _dot_nt(winT_ref[...], x_ref[...].astype(BF16))
    q_lat, kv_lat, k_rope = lat[:ql], lat[ql:ql + kvl], lat[ql + kvl:]

    def rms(v, g):
        return (v * lax.rsqrt(jnp.mean(v * v, axis=0, keepdims=True) + RMS_EPS) * g).astype(BF16)

    q = _dot(wqT_ref[...], rms(q_lat, gq_ref[...]))
    kv = _dot(wkvT_ref[...], rms(kv_lat, gkv_ref[...]))
    ang = invf_ref[...] * pos_ref[0].astype(F32)
    cos, sin = jnp.cos(ang), jnp.sin(ang)

    def rope(t1, t2):
        return t1 * cos - t2 * sin, t1 * sin + t2 * cos

    scale = dq ** -0.5 * math.log2(math.e)
    ones_rows = (lax.broadcasted_iota(jnp.int32, (16, lat.shape[1]), 0) == 0).astype(BF16)
    k1, k2 = rope(k_rope[:half], k_rope[half:])
    k1, k2 = k1.astype(BF16), k2.astype(BF16)
    for hh in range(MLA_HEADS):
        qh = q[hh * dq:(hh + 1) * dq]
        q1, q2 = rope(qh[MLA_NOPE:MLA_NOPE + half], qh[MLA_NOPE + half:])
        qT_ref[0, hh, 0:MLA_NOPE, :] = (qh[:MLA_NOPE] * scale).astype(BF16)
        qT_ref[0, hh, MLA_NOPE:MLA_NOPE + half, :] = (q1 * scale).astype(BF16)
        qT_ref[0, hh, MLA_NOPE + half:dq, :] = (q2 * scale).astype(BF16)
        kvh = kv[hh * (MLA_NOPE + MLA_V):(hh + 1) * (MLA_NOPE + MLA_V)]
        kT_ref[0, hh, 0:MLA_NOPE, :] = kvh[:MLA_NOPE].astype(BF16)
        kT_ref[0, hh, MLA_NOPE:MLA_NOPE + half, :] = k1
        kT_ref[0, hh, MLA_NOPE + half:dq, :] = k2
        vT_ref[0, hh, 0, 0:MLA_V, :] = kvh[MLA_NOPE:].astype(BF16)
        vT_ref[0, hh, 0, MLA_V:, :] = ones_rows


def _attn_kernel(qT_ref, k_ref, vT_ref, o_ref, s0_sc, s1_sc, p0_sc, p1_sc, a0_sc, a1_sc, m_sc, acc_sc,
                 *, nkv, tqs):
    dv = MLA_V
    s_sc, p_sc, a_sc = (s0_sc, s1_sc), (p0_sc, p1_sc), (a0_sc, a1_sc)
    tq = m_sc.shape[1]
    m_sc[...] = jnp.full_like(m_sc, -jnp.inf)
    acc_sc[...] = jnp.zeros_like(acc_sc)

    def scores(j, par):
        s_sc[par][...] = _dot(k_ref[0, 0, j], qT_ref[0, 0])

    def softmax(par):
        for c in range(tq // tqs):
            cols = slice(c * tqs, (c + 1) * tqs)
            m_prev = m_sc[:, cols]
            m_new = jnp.maximum(m_prev, jnp.max(s_sc[par][:, cols], axis=0, keepdims=True))
            a_sc[par][:, cols] = jnp.exp2(m_prev - m_new)
            p_sc[par][:, cols] = jnp.exp2(s_sc[par][:, cols] - m_new).astype(BF16)
            m_sc[:, cols] = m_new

    def pv(j, par):
        acc_sc[...] = a_sc[par][...] * acc_sc[...] + _dot(vT_ref[0, 0, j], p_sc[par][...])

    scores(0, 0)
    scores(1, 1)
    softmax(0)

    def body(jj, carry):
        j = 2 * jj
        scores(j, 0)
        softmax(1)
        pv(j - 2, 0)
        scores(j + 1, 1)
        softmax(0)
        pv(j - 1, 1)
        return carry

    lax.fori_loop(1, nkv // 2, body, 0)
    softmax(1)
    pv(nkv - 2, 0)
    pv(nkv - 1, 1)
    acc = acc_sc[...]
    o_ref[0, 0] = (acc[:dv] / acc[dv:dv + 1]).astype(o_ref.dtype)


def _mla_out_kernel(oT_ref, h_ref, wo_ref, g_ref, b_ref, o_ref):
    oT = oT_ref[0]
    oT = oT.reshape(oT.shape[0] * oT.shape[1], oT.shape[2])
    m = _dot_tn(oT, wo_ref[...])
    o_ref[...] = _layer_norm(DEEPNORM_ALPHA * h_ref[...] + m, g_ref[...], b_ref[...])


def _mla_mixer_ln(h, positions, bsz, seq, w_in, q_norm_g, w_q_b, kv_norm_g, w_kv_b, w_out, ln_g, ln_b,
                  tm=512, tq=1024, tqs=512):
    t, d = h.shape
    nh, dq, dv = MLA_HEADS, MLA_NOPE + MLA_ROPE, MLA_V
    half = MLA_ROPE // 2
    inv_freq = 1.0 / (ROPE_THETA ** (jnp.arange(half, dtype=F32) * (2.0 / MLA_ROPE)))
    nlat = w_in.shape[1]
    nt = seq // tm
    head = lambda n: pl.BlockSpec((1, nh, n, tm), lambda b, i: (b, 0, 0, i))
    qT, kT, vT = pl.pallas_call(
        _mla_proj_kernel,
        grid=(bsz, nt),
        in_specs=[pl.BlockSpec((tm, d), lambda b, i: (b * nt + i, 0)),
                  pl.BlockSpec((1, 1, tm), lambda b, i: (b, 0, i)),
                  _const_spec((nlat, d)), _const_spec((MLA_Q_LORA, 1)), _const_spec((MLA_KV_LORA, 1)),
                  _const_spec((nh * dq, MLA_Q_LORA)), _const_spec((nh * (MLA_NOPE + dv), MLA_KV_LORA)),
                  _const_spec((half, 1))],
        out_specs=[head(dq), head(dq),
                   pl.BlockSpec((1, nh, 1, dv + 16, tm), lambda b, i: (b, 0, i, 0, 0))],
        out_shape=[jax.ShapeDtypeStruct((bsz, nh, dq, seq), BF16),
                   jax.ShapeDtypeStruct((bsz, nh, dq, seq), BF16),
                   jax.ShapeDtypeStruct((bsz, nh, nt, dv + 16, tm), BF16)],
        compiler_params=_params("parallel", "parallel"),
        name="mla_proj",
    )(h, positions.reshape(bsz, 1, seq), w_in.T.astype(BF16), q_norm_g.astype(F32).reshape(-1, 1),
      kv_norm_g.astype(F32).reshape(-1, 1), w_q_b.T.astype(BF16), w_kv_b.T.astype(BF16),
      inv_freq.reshape(half, 1))

    tk, nkv = tm, nt
    tq = min(tq, seq)
    kc = jnp.swapaxes(kT, 2, 3).reshape(bsz, nh, nkv, tk, dq)
    oT = pl.pallas_call(
        functools.partial(_attn_kernel, nkv=nkv, tqs=tqs),
        grid=(bsz, nh, seq // tq),
        in_specs=[pl.BlockSpec((1, 1, dq, tq), lambda b, hh, i: (b, hh, 0, i)),
                  pl.BlockSpec((1, 1, nkv, tk, dq), lambda b, hh, i: (b, hh, 0, 0, 0)),
                  pl.BlockSpec((1, 1, nkv, dv + 16, tk), lambda b, hh, i: (b, hh, 0, 0, 0))],
        out_specs=pl.BlockSpec((1, 1, dv, tq), lambda b, hh, i: (b, hh, 0, i)),
        out_shape=jax.ShapeDtypeStruct((bsz, nh, dv, seq), BF16),
        scratch_shapes=[pltpu.VMEM((tk, tq), F32), pltpu.VMEM((tk, tq), F32),
                        pltpu.VMEM((tk, tq), BF16), pltpu.VMEM((tk, tq), BF16),
                        pltpu.VMEM((1, tq), F32), pltpu.VMEM((1, tq), F32),
                        pltpu.VMEM((1, tq), F32), pltpu.VMEM((dv + 16, tq), F32)],
        compiler_params=_params("parallel", "parallel", "arbitrary"),
        name="mla_attn",
    )(qT, kc, vT)

    return pl.pallas_call(
        _mla_out_kernel,
        grid=(bsz, nt),
        in_specs=[pl.BlockSpec((1, nh, dv, tm), lambda b, i: (b, 0, 0, i)),
                  pl.BlockSpec((tm, d), lambda b, i: (b * nt + i, 0)),
                  _const_spec((nh * dv, d)), _const_spec((1, d)), _const_spec((1, d))],
        out_specs=pl.BlockSpec((tm, d), lambda b, i: (b * nt + i, 0)),
        out_shape=jax.ShapeDtypeStruct((t, d), F32),
        compiler_params=_params("parallel", "parallel"),
        name="mla_out",
    )(oT, h, w_out.astype(BF16), ln_g.reshape(1, d), ln_b.reshape(1, d))


def kernel(x, positions, ln_mix_g, ln_mix_b, ln_ffn_g, ln_ffn_b, ffn_w_in, ffn_w_out, s5_w_in, s5_lam_re, s5_lam_im, s5_log_step, s5_b_re, s5_b_im, s5_c_re, s5_c_im, s5_d, s5_w_glu, s5_b_glu, s5_w_out, hg_w_in, hg_lower_bound, hg_norm_g, hg_w_out, mla_w_in, mla_q_norm_g, mla_w_q_b, mla_kv_norm_g, mla_w_kv_b, mla_w_out):
    bsz, seq, d = x.shape
    lbs = jax.nn.softmax(hg_lower_bound.astype(F32), axis=0)
    lbs = jnp.cumsum(lbs, axis=0) - lbs[0]
    h = x.reshape(bsz * seq, d)
    for layer in range(DEPTH):
        kind = layer % N_MIXERS
        slot = layer // N_MIXERS
        lg, lb = ln_mix_g[layer], ln_mix_b[layer]
        if kind == 0:
            h = _s5_mixer_ln(h, bsz, seq, s5_w_in[slot], s5_lam_re[slot], s5_lam_im[slot], s5_log_step[slot],
                             s5_b_re[slot], s5_b_im[slot], s5_c_re[slot], s5_c_im[slot], s5_d[slot],
                             s5_w_glu[slot], s5_b_glu[slot], s5_w_out[slot], lg, lb)
        elif kind == 1:
            h = _hg_mixer_ln(h, bsz, seq, hg_w_in[slot], lbs[layer], hg_norm_g[slot], hg_w_out[slot], lg, lb)
        else:
            h = _mla_mixer_ln(h, positions, bsz, seq, mla_w_in[slot], mla_q_norm_g[slot], mla_w_q_b[slot],
                              mla_kv_norm_g[slot], mla_w_kv_b[slot], mla_w_out[slot], lg, lb)
        h = _ffn_ln(h, ffn_w_in[layer], ffn_w_out[layer], ln_ffn_g[layer], ln_ffn_b[layer])
    return h.reshape(bsz, seq, d)
```

```python
import functools
import math

import jax
import jax.numpy as jnp
from jax import lax
from jax.experimental import pallas as pl
from jax.experimental.pallas import tpu as pltpu

F32 = jnp.float32
BF16 = jnp.bfloat16

DEPTH = 4
N_MIXERS = 3
S5_GROUP = 16
S5_STATE = 64
S5_CHUNK = 64
S5_PW_ROWS = 72
HG_HEAD_DIM = 128
HG_CHUNK = 128
HG_BLOCK = 512
MLA_HEADS = 8
MLA_NOPE = 128
MLA_ROPE = 64
MLA_V = 128
MLA_QK_PAD = 256
MLA_Q_LORA = 384
MLA_KV_LORA = 256
ROPE_THETA = 10000.0
DEEPNORM_ALPHA = (2 * DEPTH) ** 0.25
LN_EPS = 1e-5
RMS_EPS = 1e-6
FFN_CHUNK = 512
VMEM_LIMIT = 56 * 1024 * 1024


def _params(*sem):
    return pltpu.CompilerParams(dimension_semantics=sem, vmem_limit_bytes=VMEM_LIMIT)


def _dot(a, b):
    return jnp.dot(a, b, preferred_element_type=F32)


def _dot_nt(a, b):
    return lax.dot_general(a, b, (((1,), (1,)), ((), ())), preferred_element_type=F32)


def _dot_tn(a, b):
    return lax.dot_general(a, b, (((0,), (0,)), ((), ())), preferred_element_type=F32)


def _layer_norm(y, g, b):
    mu = jnp.mean(y, axis=-1, keepdims=True)
    d = y - mu
    var = jnp.mean(d * d, axis=-1, keepdims=True)
    return d * lax.rsqrt(var + LN_EPS) * g + b


def _const_spec(shape):
    return pl.BlockSpec(shape, lambda *_: (0,) * len(shape))


def _matmul_kernel(x_ref, w_ref, o_ref):
    o_ref[...] = _dot(x_ref[...].astype(BF16), w_ref[...]).astype(o_ref.dtype)


def _matmul(x, w, tm, out_dtype=F32):
    t, k = x.shape
    n = w.shape[1]
    return pl.pallas_call(
        _matmul_kernel,
        grid=(t // tm,),
        in_specs=[pl.BlockSpec((tm, k), lambda i: (i, 0)), _const_spec((k, n))],
        out_specs=pl.BlockSpec((tm, n), lambda i: (i, 0)),
        out_shape=jax.ShapeDtypeStruct((t, n), out_dtype),
        compiler_params=_params("parallel"),
        name="proj",
    )(x, w)


def _ffn_kernel(x_ref, win_ref, wout_ref, g_ref, b_ref, o_ref, *, hid):
    x = x_ref[...]
    xb = x.astype(BF16)
    acc = DEEPNORM_ALPHA * x
    for lo in range(0, hid, FFN_CHUNK):
        w = min(FFN_CHUNK, hid - lo)
        gate = _dot(xb, win_ref[:, lo:lo + w])
        up = _dot(xb, win_ref[:, hid + lo:hid + lo + w])
        mid = (gate * jax.nn.sigmoid(gate) * up).astype(BF16)
        acc = acc + _dot(mid, wout_ref[lo:lo + w, :])
    o_ref[...] = _layer_norm(acc, g_ref[...], b_ref[...])


def _ffn_ln(x, w_in, w_out, g, b, tm=512):
    t, d = x.shape
    hid = w_out.shape[0]
    return pl.pallas_call(
        functools.partial(_ffn_kernel, hid=hid),
        grid=(t // tm,),
        in_specs=[pl.BlockSpec((tm, d), lambda i: (i, 0)),
                  pl.BlockSpec((d, 2 * hid), lambda i: (0, 0), pipeline_mode=pl.Buffered(1)),
                  pl.BlockSpec((hid, d), lambda i: (0, 0), pipeline_mode=pl.Buffered(1)),
                  _const_spec((1, d)), _const_spec((1, d))],
        out_specs=pl.BlockSpec((tm, d), lambda i: (i, 0)),
        out_shape=jax.ShapeDtypeStruct((t, d), F32),
        compiler_params=_params("parallel"),
        name="ffn_ln",
    )(x, w_in.astype(BF16), w_out.astype(BF16), g.reshape(1, d), b.reshape(1, d))


def _s5_tables(lam_re, lam_im, log_step, b_re, b_im, c_re, c_im, nlev):
    L = S5_CHUNK
    lam = lax.complex(lam_re.astype(F32), lam_im.astype(F32))
    step = jnp.exp(log_step.astype(F32))[..., None]
    lam_dt = lam * step
    lam_bar = jnp.exp(lam_dt)
    b_bar = ((lam_bar - 1.0) / lam)[..., None] * lax.complex(b_re.astype(F32), b_im.astype(F32))
    g, p = lam.shape[1], lam.shape[2]
    taus = jnp.arange(S5_PW_ROWS, dtype=F32)
    pw = jnp.exp(lam_dt[:, None] * taus[None, :, None, None])
    pr, pi = jnp.real(pw), jnp.imag(pw)
    cat = lambda a, b: jnp.concatenate([a, b], axis=-1)
    pw_tab = jnp.stack([cat(pr, pr), cat(pi, pi)], axis=1)
    pw_tab = pw_tab.transpose(3, 0, 1, 2, 4).reshape(g, 4, S5_PW_ROWS, 2 * p)
    bt = jnp.swapaxes(b_bar, 2, 3)
    bre, bim = jnp.real(bt), jnp.imag(bt)
    cre, cim = c_re.astype(F32), c_im.astype(F32)
    b2 = cat(bre, bim)
    bc_tab = jnp.stack([b2, cat(-bim, bre), cat(cre, -cim), cat(-cim, -cre)], axis=2)
    bc_tab = bc_tab.transpose(1, 0, 2, 3, 4).reshape(g, 8, S5_GROUP, 2 * p)
    bmat = cat(b2[0], b2[1]).astype(BF16)
    k2 = (L * 2.0 ** jnp.arange(nlev, dtype=F32))
    a = jnp.exp(lam_dt[:, None] * k2[None, :, None, None])
    ar, ai = jnp.real(a), jnp.imag(a)
    p1 = cat(ar, ar).transpose(2, 1, 0, 3).reshape(g, nlev, 4 * p)
    p2 = cat(-ai, ai).transpose(2, 1, 0, 3).reshape(g, nlev, 4 * p)
    return pw_tab, bc_tab, bmat, p1, p2


def _s5_core_kernel(u_ref, pw_ref, bc_ref, bm_ref, p1_ref, p2_ref, y_ref, e_sc, gt_sc, gs_sc, t_sc, *, nc, nlev):
    L, hh, two_p = S5_CHUNK, S5_GROUP, 2 * S5_STATE
    lh = L * hh

    def blk(d, kind, tau):
        base = 4 * d + 2 * kind
        return (bc_ref[0, base] * pw_ref[0, 2 * d, tau:tau + 1, :]
                + bc_ref[0, base + 1] * pw_ref[0, 2 * d + 1, tau:tau + 1, :])

    zero = jnp.zeros((hh, two_p), BF16)
    for i in range(L):
        rows = slice(i * hh, (i + 1) * hh)
        e_sc[rows, :two_p] = blk(0, 0, L - 1 - i).astype(BF16)
        e_sc[rows, two_p:] = blk(1, 0, i).astype(BF16)
        gt_sc[rows, :two_p] = blk(0, 1, i + 1).astype(BF16)
        gt_sc[rows, two_p:] = blk(1, 1, L - i).astype(BF16)
    for tp in range(2 * L):
        rows = slice(tp * hh, (tp + 1) * hh)
        lag = tp - (L - 1)
        gs_sc[rows, :two_p] = blk(0, 1, lag).astype(BF16) if 0 <= lag < L else zero
        gs_sc[rows, two_p:] = blk(1, 1, -lag).astype(BF16) if 0 <= -lag < L else zero
    strip = _dot_nt(bm_ref[0], gs_sc[...])
    for r in range(128 // hh):
        rot = strip if r == 0 else pltpu.roll(strip, 2 * lh - r * hh, 1)
        for i in range(L):
            a, rr = divmod(L - 1 - i, 128 // hh)
            if rr == r:
                t_sc[i * hh:(i + 1) * hh, :] = rot[:, 128 * a:128 * a + lh].astype(BF16)

    u = u_ref[0]
    n = u.shape[0]
    y = _dot(u, t_sc[...])
    s = _dot(u, e_sc[...])
    cidx = lax.broadcasted_iota(jnp.int32, (n, two_p), 0) & (nc - 1)

    def cmul(x, k, d):
        a1 = p1_ref[0, k:k + 1, d * two_p:(d + 1) * two_p]
        a2 = p2_ref[0, k:k + 1, d * two_p:(d + 1) * two_p]
        return x * a1 + pltpu.roll(x, S5_STATE, 1) * a2

    xf, xb = s[:, :two_p], s[:, two_p:]
    for k in range(nlev):
        sh = 1 << k
        xf = xf + jnp.where(cidx >= sh, cmul(pltpu.roll(xf, sh, 0), k, 0), 0.0)
        xb = xb + jnp.where(cidx < nc - sh, cmul(pltpu.roll(xb, n - sh, 0), k, 1), 0.0)
    hf = jnp.where(cidx >= 1, pltpu.roll(xf, 1, 0), 0.0)
    hb = jnp.where(cidx < nc - 1, pltpu.roll(xb, n - 1, 0), 0.0)
    hcat = jnp.concatenate([hf, hb], axis=1).astype(BF16)
    y_ref[0] = y + _dot_nt(hcat, gt_sc[...])


def _s5_core(ub, tables, nc, nlev):
    g, n, lh = ub.shape
    two_p = 2 * S5_STATE
    blk = lambda a: pl.BlockSpec((1,) + a.shape[1:], lambda i: (i,) + (0,) * (a.ndim - 1))
    return pl.pallas_call(
        functools.partial(_s5_core_kernel, nc=nc, nlev=nlev),
        grid=(g,),
        in_specs=[blk(ub)] + [blk(a) for a in tables],
        out_specs=pl.BlockSpec((1, n, lh), lambda i: (i, 0, 0)),
        out_shape=jax.ShapeDtypeStruct((g, n, lh), F32),
        scratch_shapes=[pltpu.VMEM((lh, 2 * two_p), BF16), pltpu.VMEM((lh, 2 * two_p), BF16),
                        pltpu.VMEM((2 * lh, 2 * two_p), BF16), pltpu.VMEM((lh, lh), BF16)],
        compiler_params=_params("parallel"),
        name="s5_core",
    )(ub, *tables)


def _s5_post_kernel(y_ref, u_ref, h_ref, d_ref, wg_ref, bg_ref, wo_ref, g_ref, b_ref, o_ref):
    y = y_ref[...] + d_ref[...] * u_ref[...]
    y = jax.nn.gelu(y)
    z = _dot(y.astype(BF16), wg_ref[...]) + bg_ref[...]
    y = y * jax.nn.sigmoid(z)
    m = _dot(y.astype(BF16), wo_ref[...])
    o_ref[...] = _layer_norm(DEEPNORM_ALPHA * h_ref[...] + m, g_ref[...], b_ref[...])


def _s5_mixer_ln(h, bsz, seq, w_in, lam_re, lam_im, log_step, b_re, b_im, c_re, c_im,
                 d_skip, w_glu, b_glu, w_out, ln_g, ln_b, tm=512):
    t, d = h.shape
    L = S5_CHUNK
    nc = seq // L
    nlev = max(1, (nc - 1).bit_length())
    assert nc & (nc - 1) == 0
    g = d // S5_GROUP
    u = _matmul(h, w_in.astype(BF16), tm)
    ub = u.astype(BF16).reshape(bsz * nc, L, g, S5_GROUP).transpose(2, 0, 1, 3).reshape(g, bsz * nc, L * S5_GROUP)
    tables = _s5_tables(lam_re, lam_im, log_step, b_re, b_im, c_re, c_im, nlev)
    y = _s5_core(ub, tables, nc, nlev)
    y = y.reshape(g, bsz * nc, L, S5_GROUP).transpose(1, 2, 0, 3).reshape(t, d)
    row = lambda v: v.reshape(1, d)
    tile = pl.BlockSpec((tm, d), lambda i: (i, 0))
    return pl.pallas_call(
        _s5_post_kernel,
        grid=(t // tm,),
        in_specs=[tile, tile, tile, _const_spec((1, d)), _const_spec((d, d)), _const_spec((1, d)),
                  _const_spec((d, d)), _const_spec((1, d)), _const_spec((1, d))],
        out_specs=tile,
        out_shape=jax.ShapeDtypeStruct((t, d), F32),
        compiler_params=_params("parallel"),
        name="s5_post",
    )(y, u, h, row(d_skip), w_glu.astype(BF16), row(b_glu), w_out.astype(BF16), row(ln_g), row(ln_b))


def _hg_masks(c):
    nlev = c.bit_length() - 1
    t = jnp.arange(c)[:, None]
    s = jnp.arange(c)[None, :]
    ms = []
    for lev in range(nlev):
        w = 1 << lev
        ms.append(((t & w) != 0) & ((s & w) == 0) & ((t >> (lev + 1)) == (s >> (lev + 1))))
    ms.append(t == s)
    fw = jnp.stack(ms).astype(F32)
    return jnp.stack([fw, fw.transpose(0, 2, 1)])


def _hg_gate(z, la, l1m, onem):
    ls = jnp.minimum(z, 0.0) - jnp.log1p(jnp.exp(-jnp.abs(z)))
    b = l1m + ls
    lf = jnp.maximum(la, b) + jnp.log1p(jnp.exp(-jnp.abs(la - b)))
    return lf, onem * jax.nn.sigmoid(-z)


def _hg_chunk(q, z, v, la, l1m, onem, mask_ref, d, st_ref):
    c = q.shape[0]
    nlev = c.bit_length() - 1
    q = q * jax.nn.sigmoid(q)
    lf, k = _hg_gate(z, la, l1m, onem)
    qb, kb = q.astype(BF16), k.astype(BF16)
    row = lax.broadcasted_iota(jnp.int32, lf.shape, 0)
    scores = mask_ref[d, nlev] * _dot_nt(qb, kb)
    p, tot = lf, lf
    for lev in range(nlev):
        w = 1 << lev
        odd = (row & w) != 0
        if d == 0:
            e = jnp.where(odd, p, tot - p)
        else:
            e = jnp.where(odd, p - lf, tot - p + lf)
        f = jnp.exp(e)
        scores = scores + mask_ref[d, lev] * _dot_nt((q * f).astype(BF16), (k * f).astype(BF16))
        up = pltpu.roll(tot, w, 0)
        dn = pltpu.roll(tot, c - w, 0)
        p = p + jnp.where(odd, up, 0.0)
        tot = tot + jnp.where(odd, up, dn)
    if d == 0:
        eq, ek = p, tot - p
    else:
        eq, ek = tot - p + lf, p - lf
    st = st_ref[...]
    o = _dot(scores.astype(BF16), v.astype(BF16))
    o = o + _dot_nt((q * jnp.exp(eq)).astype(BF16), st.astype(BF16))
    st_ref[...] = st * jnp.exp(tot[0:1, :]) + _dot_tn(v.astype(BF16), (k * jnp.exp(ek)).astype(BF16))
    return o


def _hg_rec_kernel(qf_ref, zf_ref, vf_ref, qb_ref, zb_ref, vb_ref, la_ref, l1m_ref, onem_ref, mask_ref,
                   of_ref, ob_ref, stf_ref, stb_ref, *, nsub):
    @pl.when(pl.program_id(2) == 0)
    def _():
        stf_ref[...] = jnp.zeros_like(stf_ref)
        stb_ref[...] = jnp.zeros_like(stb_ref)

    la, l1m, onem = la_ref[...], l1m_ref[...], onem_ref[...]
    c = HG_CHUNK

    def body(j, carry):
        lo = pl.multiple_of(j * c, c)
        hi = pl.multiple_of((nsub - 1 - j) * c, c)
        rf = pl.ds(lo, c)
        rb = pl.ds(hi, c)
        of_ref[rf, :] = _hg_chunk(qf_ref[rf, :], zf_ref[rf, :], vf_ref[rf, :], la, l1m, onem, mask_ref, 0, stf_ref)
        ob_ref[rb, :] = _hg_chunk(qb_ref[rb, :], zb_ref[rb, :], vb_ref[rb, :], la, l1m, onem, mask_ref, 1, stb_ref)
        return carry

    lax.fori_loop(0, nsub, body, 0)


def _hg_post_kernel(of_ref, ob_ref, g_ref, h_ref, ng_ref, wo_ref, lg_ref, lb_ref, o_ref):
    o = of_ref[...] + ob_ref[...]
    hd = HG_HEAD_DIM
    parts = []
    for i in range(o.shape[1] // hd):
        oh = o[:, i * hd:(i + 1) * hd]
        parts.append(oh * lax.rsqrt(jnp.mean(oh * oh, axis=-1, keepdims=True) + RMS_EPS))
    on = jnp.concatenate(parts, axis=1) * ng_ref[...]
    gate = g_ref[...]
    y = on * (gate * jax.nn.sigmoid(gate))
    m = _dot(y.astype(BF16), wo_ref[...])
    o_ref[...] = _layer_norm(DEEPNORM_ALPHA * h_ref[...] + m, lg_ref[...], lb_ref[...])


def _hg_mixer_ln(h, bsz, seq, w_in, lb, norm_g, w_out, ln_g, ln_b, tm=512):
    t, d = h.shape
    hd = HG_HEAD_DIM
    nh = d // hd
    blk = min(HG_BLOCK, seq)
    nblk = seq // blk
    proj = _matmul(h, w_in.astype(BF16), tm)
    lb = lb.astype(F32).reshape(1, d)
    la, l1m, onem = jnp.log(lb), jnp.log1p(-lb), 1.0 - lb
    masks = _hg_masks(HG_CHUNK)

    def fw(col):
        return pl.BlockSpec((blk, hd), lambda b, hh, i: (b * nblk + i, col * nh + hh))

    def bw(col):
        return pl.BlockSpec((blk, hd), lambda b, hh, i: (b * nblk + nblk - 1 - i, col * nh + hh))

    vec = pl.BlockSpec((1, hd), lambda b, hh, i: (0, hh))
    o_fw, o_bw = pl.pallas_call(
        functools.partial(_hg_rec_kernel, nsub=blk // HG_CHUNK),
        grid=(bsz, nh, nblk),
        in_specs=[fw(0), fw(1), fw(3), bw(0), bw(2), bw(3), vec, vec, vec, _const_spec(masks.shape)],
        out_specs=[pl.BlockSpec((blk, hd), lambda b, hh, i: (b * nblk + i, hh)),
                   pl.BlockSpec((blk, hd), lambda b, hh, i: (b * nblk + nblk - 1 - i, hh))],
        out_shape=[jax.ShapeDtypeStruct((t, d), F32)] * 2,
        scratch_shapes=[pltpu.VMEM((hd, hd), F32), pltpu.VMEM((hd, hd), F32)],
        compiler_params=_params("parallel", "parallel", "arbitrary"),
        name="hg_rec",
    )(proj, proj, proj, proj, proj, proj, la, l1m, onem, masks)

    row = lambda v: v.reshape(1, d)
    tile = pl.BlockSpec((tm, d), lambda i: (i, 0))
    return pl.pallas_call(
        _hg_post_kernel,
        grid=(t // tm,),
        in_specs=[tile, tile, pl.BlockSpec((tm, d), lambda i: (i, 4)), tile, _const_spec((1, d)),
                  _const_spec((d, d)), _const_spec((1, d)), _const_spec((1, d))],
        out_specs=tile,
        out_shape=jax.ShapeDtypeStruct((t, d), F32),
        compiler_params=_params("parallel"),
        name="hg_post",
    )(o_fw, o_bw, proj, h, jnp.tile(norm_g.astype(F32), nh).reshape(1, d), w_out.astype(BF16), row(ln_g), row(ln_b))


def _mla_proj_kernel(x_ref, pos_ref, winT_ref, gq_ref, gkv_ref, wqT_ref, wkvT_ref, invf_ref,
                     qT_ref, k_ref, vT_ref):
    ql, kvl, half = MLA_Q_LORA, MLA_KV_LORA, MLA_ROPE // 2
    dq = MLA_NOPE + MLA_ROPE
    lat = _dot_nt(winT_ref[...], x_ref[...].astype(BF16))
    q_lat, kv_lat, k_rope = lat[:ql], lat[ql:ql + kvl], lat[ql + kvl:]

    def rms(v, g):
        return (v * lax.rsqrt(jnp.mean(v * v, axis=0, keepdims=True) + RMS_EPS) * g).astype(BF16)

    q = _dot(wqT_ref[...], rms(q_lat, gq_ref[...]))
    kv = _dot(wkvT_ref[...], rms(kv_lat, gkv_ref[...]))
    ang = invf_ref[...] * pos_ref[0].astype(F32)
    cos, sin = jnp.cos(ang), jnp.sin(ang)

    def rope(t1, t2):
        return t1 * cos - t2 * sin, t1 * sin + t2 * cos

    scale = dq ** -0.5 * math.log2(math.e)
    ones_rows = (lax.broadcasted_iota(jnp.int32, (16, lat.shape[1]), 0) == 0).astype(BF16)
    k1, k2 = rope(k_rope[:half], k_rope[half:])
    pad = jnp.zeros((MLA_QK_PAD - dq, lat.shape[1]), F32)
    k_tail = jnp.concatenate([k1, k2, pad], axis=0).T.astype(BF16)
    for hh in range(MLA_HEADS):
        qh = q[hh * dq:(hh + 1) * dq]
        q1, q2 = rope(qh[MLA_NOPE:MLA_NOPE + half], qh[MLA_NOPE + half:])
        qT_ref[0, hh, 0:MLA_NOPE, :] = (qh[:MLA_NOPE] * scale).astype(BF16)
        qT_ref[0, hh, MLA_NOPE:MLA_NOPE + half, :] = (q1 * scale).astype(BF16)
        qT_ref[0, hh, MLA_NOPE + half:dq, :] = (q2 * scale).astype(BF16)
        qT_ref[0, hh, dq:, :] = pad.astype(BF16)
        kvh = kv[hh * (MLA_NOPE + MLA_V):(hh + 1) * (MLA_NOPE + MLA_V)]
        k_ref[0, hh, 0, :, 0:MLA_NOPE] = kvh[:MLA_NOPE].T.astype(BF16)
        k_ref[0, hh, 0, :, MLA_NOPE:] = k_tail
        vT_ref[0, hh, 0, 0:MLA_V, :] = kvh[MLA_NOPE:].astype(BF16)
        vT_ref[0, hh, 0, MLA_V:, :] = ones_rows


def _attn_kernel(qT_ref, k_ref, vT_ref, o_ref, s0_sc, s1_sc, p0_sc, p1_sc, a0_sc, a1_sc, m_sc, acc_sc,
                 *, nkv, tqs):
    dv = MLA_V
    s_sc, p_sc, a_sc = (s0_sc, s1_sc), (p0_sc, p1_sc), (a0_sc, a1_sc)
    tq = m_sc.shape[1]
    m_sc[...] = jnp.full_like(m_sc, -jnp.inf)
    acc_sc[...] = jnp.zeros_like(acc_sc)

    def scores(j, par):
        s_sc[par][...] = _dot(k_ref[0, 0, j], qT_ref[0, 0])

    def softmax(par):
        for c in range(tq // tqs):
            cols = slice(c * tqs, (c + 1) * tqs)
            m_prev = m_sc[:, cols]
            m_new = jnp.maximum(m_prev, jnp.max(s_sc[par][:, cols], axis=0, keepdims=True))
            a_sc[par][:, cols] = jnp.exp2(m_prev - m_new)
            p_sc[par][:, cols] = jnp.exp2(s_sc[par][:, cols] - m_new).astype(BF16)
            m_sc[:, cols] = m_new

    def pv(j, par):
        acc_sc[...] = a_sc[par][...] * acc_sc[...] + _dot(vT_ref[0, 0, j], p_sc[par][...])

    scores(0, 0)
    scores(1, 1)
    softmax(0)

    def body(jj, carry):
        j = 2 * jj
        scores(j, 0)
        softmax(1)
        pv(j - 2, 0)
        scores(j + 1, 1)
        softmax(0)
        pv(j - 1, 1)
        return carry

    lax.fori_loop(1, nkv // 2, body, 0)
    softmax(1)
    pv(nkv - 2, 0)
    pv(nkv - 1, 1)
    acc = acc_sc[...]
    o_ref[0, 0] = (acc[:dv] / acc[dv:dv + 1]).astype(o_ref.dtype)


def _mla_out_kernel(oT_ref, h_ref, wo_ref, g_ref, b_ref, o_ref):
    oT = oT_ref[0]
    oT = oT.reshape(oT.shape[0] * oT.shape[1], oT.shape[2])
    m = _dot_tn(oT, wo_ref[...])
    o_ref[...] = _layer_norm(DEEPNORM_ALPHA * h_ref[...] + m, g_ref[...], b_ref[...])


def _mla_mixer_ln(h, positions, bsz, seq, w_in, q_norm_g, w_q_b, kv_norm_g, w_kv_b, w_out, ln_g, ln_b,
                  tm=512, tq=1024, tqs=512):
    t, d = h.shape
    nh, dq, dv = MLA_HEADS, MLA_NOPE + MLA_ROPE, MLA_V
    half = MLA_ROPE // 2
    inv_freq = 1.0 / (ROPE_THETA ** (jnp.arange(half, dtype=F32) * (2.0 / MLA_ROPE)))
    nlat = w_in.shape[1]
    nt = seq // tm
    dp = MLA_QK_PAD
    qT, kc, vT = pl.pallas_call(
        _mla_proj_kernel,
        grid=(bsz, nt),
        in_specs=[pl.BlockSpec((tm, d), lambda b, i: (b * nt + i, 0)),
                  pl.BlockSpec((1, 1, tm), lambda b, i: (b, 0, i)),
                  _const_spec((nlat, d)), _const_spec((MLA_Q_LORA, 1)), _const_spec((MLA_KV_LORA, 1)),
                  _const_spec((nh * dq, MLA_Q_LORA)), _const_spec((nh * (MLA_NOPE + dv), MLA_KV_LORA)),
                  _const_spec((half, 1))],
        out_specs=[pl.BlockSpec((1, nh, dp, tm), lambda b, i: (b, 0, 0, i)),
                   pl.BlockSpec((1, nh, 1, tm, dp), lambda b, i: (b, 0, i, 0, 0)),
                   pl.BlockSpec((1, nh, 1, dv + 16, tm), lambda b, i: (b, 0, i, 0, 0))],
        out_shape=[jax.ShapeDtypeStruct((bsz, nh, dp, seq), BF16),
                   jax.ShapeDtypeStruct((bsz, nh, nt, tm, dp), BF16),
                   jax.ShapeDtypeStruct((bsz, nh, nt, dv + 16, tm), BF16)],
        compiler_params=_params("parallel", "parallel"),
        name="mla_proj",
    )(h, positions.reshape(bsz, 1, seq), w_in.T.astype(BF16), q_norm_g.astype(F32).reshape(-1, 1),
      kv_norm_g.astype(F32).reshape(-1, 1), w_q_b.T.astype(BF16), w_kv_b.T.astype(BF16),
      inv_freq.reshape(half, 1))

    tk, nkv = tm, nt
    tq = min(tq, seq)
    oT = pl.pallas_call(
        functools.partial(_attn_kernel, nkv=nkv, tqs=tqs),
        grid=(bsz, nh, seq // tq),
        in_specs=[pl.BlockSpec((1, 1, dp, tq), lambda b, hh, i: (b, hh, 0, i)),
                  pl.BlockSpec((1, 1, nkv, tk, dp), lambda b, hh, i: (b, hh, 0, 0, 0)),
                  pl.BlockSpec((1, 1, nkv, dv + 16, tk), lambda b, hh, i: (b, hh, 0, 0, 0))],
        out_specs=pl.BlockSpec((1, 1, dv, tq), lambda b, hh, i: (b, hh, 0, i)),
        out_shape=jax.ShapeDtypeStruct((bsz, nh, dv, seq), BF16),
        scratch_shapes=[pltpu.VMEM((tk, tq), F32), pltpu.VMEM((tk, tq), F32),
                        pltpu.VMEM((tk, tq), BF16), pltpu.VMEM((tk, tq), BF16),
                        pltpu.VMEM((1, tq), F32), pltpu.VMEM((1, tq), F32),
                        pltpu.VMEM((1, tq), F32), pltpu.VMEM((dv + 16, tq), F32)],
        compiler_params=_params("parallel", "parallel", "arbitrary"),
        name="mla_attn",
    )(qT, kc, vT)

    return pl.pallas_call(
        _mla_out_kernel,
        grid=(bsz, nt),
        in_specs=[pl.BlockSpec((1, nh, dv, tm), lambda b, i: (b, 0, 0, i)),
                  pl.BlockSpec((tm, d), lambda b, i: (b * nt + i, 0)),
                  _const_spec((nh * dv, d)), _const_spec((1, d)), _const_spec((1, d))],
        out_specs=pl.BlockSpec((tm, d), lambda b, i: (b * nt + i, 0)),
        out_shape=jax.ShapeDtypeStruct((t, d), F32),
        compiler_params=_params("parallel", "parallel"),
        name="mla_out",
    )(oT, h, w_out.astype(BF16), ln_g.reshape(1, d), ln_b.reshape(1, d))


def kernel(x, positions, ln_mix_g, ln_mix_b, ln_ffn_g, ln_ffn_b, ffn_w_in, ffn_w_out, s5_w_in, s5_lam_re, s5_lam_im, s5_log_step, s5_b_re, s5_b_im, s5_c_re, s5_c_im, s5_d, s5_w_glu, s5_b_glu, s5_w_out, hg_w_in, hg_lower_bound, hg_norm_g, hg_w_out, mla_w_in, mla_q_norm_g, mla_w_q_b, mla_kv_norm_g, mla_w_kv_b, mla_w_out):
    bsz, seq, d = x.shape
    lbs = jax.nn.softmax(hg_lower_bound.astype(F32), axis=0)
    lbs = jnp.cumsum(lbs, axis=0) - lbs[0]
    h = x.reshape(bsz * seq, d)
    for layer in range(DEPTH):
        kind = layer % N_MIXERS
        slot = layer // N_MIXERS
        lg, lb = ln_mix_g[layer], ln_mix_b[layer]
        if kind == 0:
            h = _s5_mixer_ln(h, bsz, seq, s5_w_in[slot], s5_lam_re[slot], s5_lam_im[slot], s5_log_step[slot],
                             s5_b_re[slot], s5_b_im[slot], s5_c_re[slot], s5_c_im[slot], s5_d[slot],
                             s5_w_glu[slot], s5_b_glu[slot], s5_w_out[slot], lg, lb)
        elif kind == 1:
            h = _hg_mixer_ln(h, bsz, seq, hg_w_in[slot], lbs[layer], hg_norm_g[slot], hg_w_out[slot], lg, lb)
        else:
            h = _mla_mixer_ln(h, positions, bsz, seq, mla_w_in[slot], mla_q_norm_g[slot], mla_w_q_b[slot],
                              mla_kv_norm_g[slot], mla_w_kv_b[slot], mla_w_out[slot], lg, lb)
        h = _ffn_ln(h, ffn_w_in[layer], ffn_w_out[layer], ln_ffn_g[layer], ln_ffn_b[layer])
    return h.reshape(bsz, seq, d)
```

```python
import functools
import math

import jax
import jax.numpy as jnp
from jax import lax
from jax.experimental import pallas as pl
from jax.experimental.pallas import tpu as pltpu

F32 = jnp.float32
BF16 = jnp.bfloat16

DEPTH = 4
N_MIXERS = 3
S5_GROUP = 16
S5_STATE = 64
S5_CHUNK = 64
S5_PW_ROWS = 72
HG_HEAD_DIM = 128
HG_CHUNK = 128
HG_BLOCK = 512
MLA_HEADS = 8
MLA_NOPE = 128
MLA_ROPE = 64
MLA_V = 128
MLA_QK_PAD = 256
MLA_Q_LORA = 384
MLA_KV_LORA = 256
ROPE_THETA = 10000.0
DEEPNORM_ALPHA = (2 * DEPTH) ** 0.25
LN_EPS = 1e-5
RMS_EPS = 1e-6
FFN_CHUNK = 512
VMEM_LIMIT = 56 * 1024 * 1024


def _params(*sem):
    return pltpu.CompilerParams(dimension_semantics=sem, vmem_limit_bytes=VMEM_LIMIT)


def _dot(a, b):
    return jnp.dot(a, b, preferred_element_type=F32)


def _dot_nt(a, b):
    return lax.dot_general(a, b, (((1,), (1,)), ((), ())), preferred_element_type=F32)


def _dot_tn(a, b):
    return lax.dot_general(a, b, (((0,), (0,)), ((), ())), preferred_element_type=F32)


def _layer_norm(y, g, b):
    mu = jnp.mean(y, axis=-1, keepdims=True)
    d = y - mu
    var = jnp.mean(d * d, axis=-1, keepdims=True)
    return d * lax.rsqrt(var + LN_EPS) * g + b


def _const_spec(shape):
    return pl.BlockSpec(shape, lambda *_: (0,) * len(shape))


def _matmul_kernel(x_ref, w_ref, o_ref):
    o_ref[...] = _dot(x_ref[...].astype(BF16), w_ref[...]).astype(o_ref.dtype)


def _matmul(x, w, tm, out_dtype=F32):
    t, k = x.shape
    n = w.shape[1]
    return pl.pallas_call(
        _matmul_kernel,
        grid=(t // tm,),
        in_specs=[pl.BlockSpec((tm, k), lambda i: (i, 0)), _const_spec((k, n))],
        out_specs=pl.BlockSpec((tm, n), lambda i: (i, 0)),
        out_shape=jax.ShapeDtypeStruct((t, n), out_dtype),
        compiler_params=_params("parallel"),
        name="proj",
    )(x, w)


def _ffn_kernel(x_ref, win_ref, wout_ref, g_ref, b_ref, o_ref, *, hid):
    x = x_ref[...]
    xb = x.astype(BF16)
    acc = DEEPNORM_ALPHA * x
    for lo in range(0, hid, FFN_CHUNK):
        w = min(FFN_CHUNK, hid - lo)
        gate = _dot(xb, win_ref[:, lo:lo + w])
        up = _dot(xb, win_ref[:, hid + lo:hid + lo + w])
        mid = (gate * jax.nn.sigmoid(gate) * up).astype(BF16)
        acc = acc + _dot(mid, wout_ref[lo:lo + w, :])
    o_ref[...] = _layer_norm(acc, g_ref[...], b_ref[...])


def _ffn_ln(x, w_in, w_out, g, b, tm=512):
    t, d = x.shape
    hid = w_out.shape[0]
    return pl.pallas_call(
        functools.partial(_ffn_kernel, hid=hid),
        grid=(t // tm,),
        in_specs=[pl.BlockSpec((tm, d), lambda i: (i, 0)),
                  pl.BlockSpec((d, 2 * hid), lambda i: (0, 0), pipeline_mode=pl.Buffered(1)),
                  pl.BlockSpec((hid, d), lambda i: (0, 0), pipeline_mode=pl.Buffered(1)),
                  _const_spec((1, d)), _const_spec((1, d))],
        out_specs=pl.BlockSpec((tm, d), lambda i: (i, 0)),
        out_shape=jax.ShapeDtypeStruct((t, d), F32),
        compiler_params=_params("parallel"),
        name="ffn_ln",
    )(x, w_in.astype(BF16), w_out.astype(BF16), g.reshape(1, d), b.reshape(1, d))


def _s5_tables(lam_re, lam_im, log_step, b_re, b_im, c_re, c_im, nlev):
    L = S5_CHUNK
    lam = lax.complex(lam_re.astype(F32), lam_im.astype(F32))
    step = jnp.exp(log_step.astype(F32))[..., None]
    lam_dt = lam * step
    lam_bar = jnp.exp(lam_dt)
    b_bar = ((lam_bar - 1.0) / lam)[..., None] * lax.complex(b_re.astype(F32), b_im.astype(F32))
    g, p = lam.shape[1], lam.shape[2]
    taus = jnp.arange(S5_PW_ROWS, dtype=F32)
    pw = jnp.exp(lam_dt[:, None] * taus[None, :, None, None])
    pr, pi = jnp.real(pw), jnp.imag(pw)
    cat = lambda a, b: jnp.concatenate([a, b], axis=-1)
    pw_tab = jnp.stack([cat(pr, pr), cat(pi, pi)], axis=1)
    pw_tab = pw_tab.transpose(3, 0, 1, 2, 4).reshape(g, 4, S5_PW_ROWS, 2 * p)
    bt = jnp.swapaxes(b_bar, 2, 3)
    bre, bim = jnp.real(bt), jnp.imag(bt)
    cre, cim = c_re.astype(F32), c_im.astype(F32)
    c2 = cat(cre, -cim)
    bc_tab = jnp.stack([cat(bre, bim), cat(-bim, bre), c2, cat(-cim, -cre)], axis=2)
    bc_tab = bc_tab.transpose(1, 0, 2, 3, 4).reshape(g, 8, S5_GROUP, 2 * p)
    cmat = cat(c2[0], c2[1]).astype(BF16)
    k2 = (L * 2.0 ** jnp.arange(nlev, dtype=F32))
    a = jnp.exp(lam_dt[:, None] * k2[None, :, None, None])
    ar, ai = jnp.real(a), jnp.imag(a)
    p1 = cat(ar, ar).transpose(2, 0, 3, 1).reshape(g, 4 * p, nlev)
    p2 = cat(-ai, ai).transpose(2, 0, 3, 1).reshape(g, 4 * p, nlev)
    return pw_tab, bc_tab, cmat, p1, p2


def _s5_core_kernel(u_ref, pw_ref, bc_ref, cm_ref, p1_ref, p2_ref, y_ref, ut_sc, e_sc, gt_sc, bs_sc, tt_sc,
                    *, nc, nlev):
    L, hh, p, two_p = S5_CHUNK, S5_GROUP, S5_STATE, 2 * S5_STATE
    lh = L * hh
    n = ut_sc.shape[1]

    def blk(d, kind, tau):
        base = 4 * d + 2 * kind
        return (bc_ref[0, base] * pw_ref[0, 2 * d, tau:tau + 1, :]
                + bc_ref[0, base + 1] * pw_ref[0, 2 * d + 1, tau:tau + 1, :])

    zero = jnp.zeros((hh, two_p), BF16)
    for i in range(L):
        rows = slice(i * hh, (i + 1) * hh)
        ut_sc[rows, :] = u_ref[:, i * n:(i + 1) * n]
        e_sc[rows, :two_p] = blk(0, 0, L - 1 - i).astype(BF16)
        e_sc[rows, two_p:] = blk(1, 0, i).astype(BF16)
        gt_sc[rows, :two_p] = blk(0, 1, i + 1).astype(BF16)
        gt_sc[rows, two_p:] = blk(1, 1, L - i).astype(BF16)
    for rho in range(2 * L):
        rows = slice(rho * hh, (rho + 1) * hh)
        lag = L - 1 - rho
        bs_sc[rows, :two_p] = blk(0, 0, lag).astype(BF16) if 0 <= lag < L else zero
        bs_sc[rows, two_p:] = blk(1, 0, -lag).astype(BF16) if 0 <= -lag < L else zero
    strip = _dot_nt(cm_ref[0], bs_sc[...])
    for r in range(128 // hh):
        rot = strip if r == 0 else pltpu.roll(strip, 2 * lh - r * hh, 1)
        for j in range(L):
            a, rr = divmod(L - 1 - j, 128 // hh)
            if rr == r:
                tt_sc[j * hh:(j + 1) * hh, :] = rot[:, 128 * a:128 * a + lh].astype(BF16)

    ut = ut_sc[...]
    y = _dot(tt_sc[...], ut)
    s = _dot_tn(e_sc[...], ut)
    cidx = lax.broadcasted_iota(jnp.int32, (two_p, n), 1) & (nc - 1)

    def cmul(x, k, d):
        a1 = p1_ref[0, d * two_p:(d + 1) * two_p, k:k + 1]
        a2 = p2_ref[0, d * two_p:(d + 1) * two_p, k:k + 1]
        return x * a1 + jnp.concatenate([x[p:], x[:p]], axis=0) * a2

    xf, xb = s[:two_p], s[two_p:]
    for k in range(nlev):
        sh = 1 << k
        xf = xf + jnp.where(cidx >= sh, cmul(pltpu.roll(xf, sh, 1), k, 0), 0.0)
        xb = xb + jnp.where(cidx < nc - sh, cmul(pltpu.roll(xb, n - sh, 1), k, 1), 0.0)
    hf = jnp.where(cidx >= 1, pltpu.roll(xf, 1, 1), 0.0)
    hb = jnp.where(cidx < nc - 1, pltpu.roll(xb, n - 1, 1), 0.0)
    y = y + _dot(gt_sc[...], jnp.concatenate([hf, hb], axis=0).astype(BF16))
    for j in range(L):
        y_ref[:, j * n:(j + 1) * n] = y[j * hh:(j + 1) * hh, :]


def _s5_core(ut, tables, n, nc, nlev):
    d, t = ut.shape
    hh, two_p = S5_GROUP, 2 * S5_STATE
    lh = S5_CHUNK * hh
    blk = lambda a: pl.BlockSpec((1,) + a.shape[1:], lambda i: (i,) + (0,) * (a.ndim - 1))
    return pl.pallas_call(
        functools.partial(_s5_core_kernel, nc=nc, nlev=nlev),
        grid=(d // hh,),
        in_specs=[pl.BlockSpec((hh, t), lambda i: (i, 0))] + [blk(a) for a in tables],
        out_specs=pl.BlockSpec((hh, t), lambda i: (i, 0)),
        out_shape=jax.ShapeDtypeStruct((d, t), F32),
        scratch_shapes=[pltpu.VMEM((lh, n), BF16),
                        pltpu.VMEM((lh, 2 * two_p), BF16), pltpu.VMEM((lh, 2 * two_p), BF16),
                        pltpu.VMEM((2 * lh, 2 * two_p), BF16), pltpu.VMEM((lh, lh), BF16)],
        compiler_params=_params("parallel"),
        name="s5_core",
    )(ut, *tables)


def _s5_in_kernel(x_ref, wT_ref, o_ref):
    o_ref[...] = _dot_nt(wT_ref[...], x_ref[...].astype(BF16)).astype(o_ref.dtype)


def _s5_post_kernel(yT_ref, x_ref, winT_ref, d_ref, wgT_ref, bg_ref, wo_ref, g_ref, b_ref, o_ref):
    x = x_ref[...]
    u = _dot_nt(winT_ref[...], x.astype(BF16))
    y = yT_ref[...] + d_ref[...] * u
    y = jax.nn.gelu(y)
    z = _dot(wgT_ref[...], y.astype(BF16)) + bg_ref[...]
    y = y * jax.nn.sigmoid(z)
    m = _dot_tn(y.astype(BF16), wo_ref[...])
    o_ref[...] = _layer_norm(DEEPNORM_ALPHA * x + m, g_ref[...], b_ref[...])


def _s5_mixer_ln(h, bsz, seq, w_in, lam_re, lam_im, log_step, b_re, b_im, c_re, c_im,
                 d_skip, w_glu, b_glu, w_out, ln_g, ln_b, tm=512):
    t, d = h.shape
    L = S5_CHUNK
    nc = seq // L
    n = bsz * nc
    nlev = max(1, (nc - 1).bit_length())
    assert nc & (nc - 1) == 0
    hp = h.reshape(n, L, d).transpose(1, 0, 2).reshape(t, d)
    w_in_t = w_in.T.astype(BF16)
    tile = pl.BlockSpec((tm, d), lambda i: (i, 0))
    tile_t = pl.BlockSpec((d, tm), lambda i: (0, i))
    ut = pl.pallas_call(
        _s5_in_kernel,
        grid=(t // tm,),
        in_specs=[tile, _const_spec((d, d))],
        out_specs=tile_t,
        out_shape=jax.ShapeDtypeStruct((d, t), BF16),
        compiler_params=_params("parallel"),
        name="s5_in",
    )(hp, w_in_t)
    tables = _s5_tables(lam_re, lam_im, log_step, b_re, b_im, c_re, c_im, nlev)
    yt = _s5_core(ut, tables, n, nc, nlev)
    row = lambda v: v.reshape(1, d)
    col = lambda v: v.astype(F32).reshape(d, 1)
    out = pl.pallas_call(
        _s5_post_kernel,
        grid=(t // tm,),
        in_specs=[tile_t, tile, _const_spec((d, d)), _const_spec((d, 1)), _const_spec((d, d)), _const_spec((d, 1)),
                  _const_spec((d, d)), _const_spec((1, d)), _const_spec((1, d))],
        out_specs=tile,
        out_shape=jax.ShapeDtypeStruct((t, d), F32),
        compiler_params=_params("parallel"),
        name="s5_post",
    )(yt, hp, w_in_t, col(d_skip), w_glu.T.astype(BF16), col(b_glu), w_out.astype(BF16), row(ln_g), row(ln_b))
    return out.reshape(L, n, d).transpose(1, 0, 2).reshape(t, d)


def _hg_masks(c):
    nlev = c.bit_length() - 1
    t = jnp.arange(c)[:, None]
    s = jnp.arange(c)[None, :]
    ms = []
    for lev in range(nlev):
        w = 1 << lev
        ms.append(((t & w) != 0) & ((s & w) == 0) & ((t >> (lev + 1)) == (s >> (lev + 1))))
    ms.append(t == s)
    fw = jnp.stack(ms).astype(F32)
    return jnp.stack([fw, fw.transpose(0, 2, 1)])


def _hg_gate(z, la, l1m, onem):
    ls = jnp.minimum(z, 0.0) - jnp.log1p(jnp.exp(-jnp.abs(z)))
    b = l1m + ls
    lf = jnp.maximum(la, b) + jnp.log1p(jnp.exp(-jnp.abs(la - b)))
    return lf, onem * jax.nn.sigmoid(-z)


def _hg_chunk(q, z, v, la, l1m, onem, mask_ref, d, st_ref):
    c = q.shape[0]
    nlev = c.bit_length() - 1
    q = q * jax.nn.sigmoid(q)
    lf, k = _hg_gate(z, la, l1m, onem)
    qb, kb = q.astype(BF16), k.astype(BF16)
    row = lax.broadcasted_iota(jnp.int32, lf.shape, 0)
    scores = mask_ref[d, nlev] * _dot_nt(qb, kb)
    p, tot = lf, lf
    for lev in range(nlev):
        w = 1 << lev
        odd = (row & w) != 0
        if d == 0:
            e = jnp.where(odd, p, tot - p)
        else:
            e = jnp.where(odd, p - lf, tot - p + lf)
        f = jnp.exp(e)
        scores = scores + mask_ref[d, lev] * _dot_nt((q * f).astype(BF16), (k * f).astype(BF16))
        up = pltpu.roll(tot, w, 0)
        dn = pltpu.roll(tot, c - w, 0)
        p = p + jnp.where(odd, up, 0.0)
        tot = tot + jnp.where(odd, up, dn)
    if d == 0:
        eq, ek = p, tot - p
    else:
        eq, ek = tot - p + lf, p - lf
    st = st_ref[...]
    o = _dot(scores.astype(BF16), v.astype(BF16))
    o = o + _dot_nt((q * jnp.exp(eq)).astype(BF16), st.astype(BF16))
    st_ref[...] = st * jnp.exp(tot[0:1, :]) + _dot_tn(v.astype(BF16), (k * jnp.exp(ek)).astype(BF16))
    return o


def _hg_rec_kernel(qf_ref, zf_ref, vf_ref, qb_ref, zb_ref, vb_ref, la_ref, l1m_ref, onem_ref, mask_ref,
                   of_ref, ob_ref, stf_ref, stb_ref, *, nsub):
    @pl.when(pl.program_id(2) == 0)
    def _():
        stf_ref[...] = jnp.zeros_like(stf_ref)
        stb_ref[...] = jnp.zeros_like(stb_ref)

    la, l1m, onem = la_ref[...], l1m_ref[...], onem_ref[...]
    c = HG_CHUNK

    def body(j, carry):
        lo = pl.multiple_of(j * c, c)
        hi = pl.multiple_of((nsub - 1 - j) * c, c)
        rf = pl.ds(lo, c)
        rb = pl.ds(hi, c)
        of_ref[rf, :] = _hg_chunk(qf_ref[rf, :], zf_ref[rf, :], vf_ref[rf, :], la, l1m, onem, mask_ref, 0, stf_ref)
        ob_ref[rb, :] = _hg_chunk(qb_ref[rb, :], zb_ref[rb, :], vb_ref[rb, :], la, l1m, onem, mask_ref, 1, stb_ref)
        return carry

    lax.fori_loop(0, nsub, body, 0)


def _hg_post_kernel(of_ref, ob_ref, g_ref, h_ref, ng_ref, wo_ref, lg_ref, lb_ref, o_ref):
    o = of_ref[...] + ob_ref[...]
    hd = HG_HEAD_DIM
    parts = []
    for i in range(o.shape[1] // hd):
        oh = o[:, i * hd:(i + 1) * hd]
        parts.append(oh * lax.rsqrt(jnp.mean(oh * oh, axis=-1, keepdims=True) + RMS_EPS))
    on = jnp.concatenate(parts, axis=1) * ng_ref[...]
    gate = g_ref[...]
    y = on * (gate * jax.nn.sigmoid(gate))
    m = _dot(y.astype(BF16), wo_ref[...])
    o_ref[...] = _layer_norm(DEEPNORM_ALPHA * h_ref[...] + m, lg_ref[...], lb_ref[...])


def _hg_mixer_ln(h, bsz, seq, w_in, lb, norm_g, w_out, ln_g, ln_b, tm=512):
    t, d = h.shape
    hd = HG_HEAD_DIM
    nh = d // hd
    blk = min(HG_BLOCK, seq)
    nblk = seq // blk
    proj = _matmul(h, w_in.astype(BF16), tm)
    lb = lb.astype(F32).reshape(1, d)
    la, l1m, onem = jnp.log(lb), jnp.log1p(-lb), 1.0 - lb
    masks = _hg_masks(HG_CHUNK)

    def fw(col):
        return pl.BlockSpec((blk, hd), lambda b, hh, i: (b * nblk + i, col * nh + hh))

    def bw(col):
        return pl.BlockSpec((blk, hd), lambda b, hh, i: (b * nblk + nblk - 1 - i, col * nh + hh))

    vec = pl.BlockSpec((1, hd), lambda b, hh, i: (0, hh))
    o_fw, o_bw = pl.pallas_call(
        functools.partial(_hg_rec_kernel, nsub=blk // HG_CHUNK),
        grid=(bsz, nh, nblk),
        in_specs=[fw(0), fw(1), fw(3), bw(0), bw(2), bw(3), vec, vec, vec, _const_spec(masks.shape)],
        out_specs=[pl.BlockSpec((blk, hd), lambda b, hh, i: (b * nblk + i, hh)),
                   pl.BlockSpec((blk, hd), lambda b, hh, i: (b * nblk + nblk - 1 - i, hh))],
        out_shape=[jax.ShapeDtypeStruct((t, d), F32)] * 2,
        scratch_shapes=[pltpu.VMEM((hd, hd), F32), pltpu.VMEM((hd, hd), F32)],
        compiler_params=_params("parallel", "parallel", "arbitrary"),
        name="hg_rec",
    )(proj, proj, proj, proj, proj, proj, la, l1m, onem, masks)

    row = lambda v: v.reshape(1, d)
    tile = pl.BlockSpec((tm, d), lambda i: (i, 0))
    return pl.pallas_call(
        _hg_post_kernel,
        grid=(t // tm,),
        in_specs=[tile, tile, pl.BlockSpec((tm, d), lambda i: (i, 4)), tile, _const_spec((1, d)),
                  _const_spec((d, d)), _const_spec((1, d)), _const_spec((1, d))],
        out_specs=tile,
        out_shape=jax.ShapeDtypeStruct((t, d), F32),
        compiler_params=_params("parallel"),
        name="hg_post",
    )(o_fw, o_bw, proj, h, jnp.tile(norm_g.astype(F32), nh).reshape(1, d), w_out.astype(BF16), row(ln_g), row(ln_b))


def _mla_proj_kernel(x_ref, pos_ref, winT_ref, gq_ref, gkv_ref, wqT_ref, wkvT_ref, invf_ref,
                     qT_ref, k_ref, vT_ref):
    ql, kvl, half = MLA_Q_LORA, MLA_KV_LORA, MLA_ROPE // 2
    dq = MLA_NOPE + MLA_ROPE
    lat = _dot_nt(winT_ref[...], x_ref[...].astype(BF16))
    q_lat, kv_lat, k_rope = lat[:ql], lat[ql:ql + kvl], lat[ql + kvl:]

    def rms(v, g):
        return (v * lax.rsqrt(jnp.mean(v * v, axis=0, keepdims=True) + RMS_EPS) * g).astype(BF16)

    q = _dot(wqT_ref[...], rms(q_lat, gq_ref[...]))
    kv = _dot(wkvT_ref[...], rms(kv_lat, gkv_ref[...]))
    ang = invf_ref[...] * pos_ref[0].astype(F32)
    cos, sin = jnp.cos(ang), jnp.sin(ang)

    def rope(t1, t2):
        return t1 * cos - t2 * sin, t1 * sin + t2 * cos

    scale = dq ** -0.5 * math.log2(math.e)
    ones_rows = (lax.broadcasted_iota(jnp.int32, (16, lat.shape[1]), 0) == 0).astype(BF16)
    k1, k2 = rope(k_rope[:half], k_rope[half:])
    pad = jnp.zeros((MLA_QK_PAD - dq, lat.shape[1]), F32)
    k_tail = jnp.concatenate([k1, k2, pad], axis=0).T.astype(BF16)
    for hh in range(MLA_HEADS):
        qh = q[hh * dq:(hh + 1) * dq]
        q1, q2 = rope(qh[MLA_NOPE:MLA_NOPE + half], qh[MLA_NOPE + half:])
        qT_ref[0, hh, 0:MLA_NOPE, :] = (qh[:MLA_NOPE] * scale).astype(BF16)
        qT_ref[0, hh, MLA_NOPE:MLA_NOPE + half, :] = (q1 * scale).astype(BF16)
        qT_ref[0, hh, MLA_NOPE + half:dq, :] = (q2 * scale).astype(BF16)
        qT_ref[0, hh, dq:, :] = pad.astype(BF16)
        kvh = kv[hh * (MLA_NOPE + MLA_V):(hh + 1) * (MLA_NOPE + MLA_V)]
        k_ref[0, hh, 0, :, 0:MLA_NOPE] = kvh[:MLA_NOPE].T.astype(BF16)
        k_ref[0, hh, 0, :, MLA_NOPE:] = k_tail
        vT_ref[0, hh, 0, 0:MLA_V, :] = kvh[MLA_NOPE:].astype(BF16)
        vT_ref[0, hh, 0, MLA_V:, :] = ones_rows


def _attn_kernel(qT_ref, k_ref, vT_ref, o_ref, s0_sc, s1_sc, p0_sc, p1_sc, a0_sc, a1_sc, t0_sc, t1_sc,
                 m_sc, acc_sc, *, nkv, tqs):
    dv = MLA_V
    s_sc, p_sc, a_sc, t_sc = (s0_sc, s1_sc), (p0_sc, p1_sc), (a0_sc, a1_sc), (t0_sc, t1_sc)
    tq = m_sc.shape[1]
    m_sc[...] = jnp.full_like(m_sc, -jnp.inf)
    acc_sc[...] = jnp.zeros_like(acc_sc)

    def scores(j, par):
        for c in range(tq // tqs):
            cols = slice(c * tqs, (c + 1) * tqs)
            s = _dot(k_ref[0, 0, j], qT_ref[0, 0, :, cols])
            s_sc[par][:, cols] = s
            t_sc[par][:, cols] = jnp.max(s, axis=0, keepdims=True)

    def softmax(par):
        for c in range(tq // tqs):
            cols = slice(c * tqs, (c + 1) * tqs)
            m_prev = m_sc[:, cols]
            m_new = jnp.maximum(m_prev, t_sc[par][:, cols])
            a_sc[par][:, cols] = jnp.exp2(m_prev - m_new)
            p_sc[par][:, cols] = jnp.exp2(s_sc[par][:, cols] - m_new).astype(BF16)
            m_sc[:, cols] = m_new

    def pv(j, par):
        acc_sc[...] = a_sc[par][...] * acc_sc[...] + _dot(vT_ref[0, 0, j], p_sc[par][...])

    scores(0, 0)
    scores(1, 1)
    softmax(0)

    def body(jj, carry):
        j = 2 * jj
        scores(j, 0)
        softmax(1)
        pv(j - 2, 0)
        scores(j + 1, 1)
        softmax(0)
        pv(j - 1, 1)
        return carry

    lax.fori_loop(1, nkv // 2, body, 0)
    softmax(1)
    pv(nkv - 2, 0)
    pv(nkv - 1, 1)
    acc = acc_sc[...]
    o_ref[0, 0] = (acc[:dv] / acc[dv:dv + 1]).astype(o_ref.dtype)


def _mla_out_kernel(oT_ref, h_ref, wo_ref, g_ref, b_ref, o_ref):
    oT = oT_ref[0]
    oT = oT.reshape(oT.shape[0] * oT.shape[1], oT.shape[2])
    m = _dot_tn(oT, wo_ref[...])
    o_ref[...] = _layer_norm(DEEPNORM_ALPHA * h_ref[...] + m, g_ref[...], b_ref[...])


def _mla_mixer_ln(h, positions, bsz, seq, w_in, q_norm_g, w_q_b, kv_norm_g, w_kv_b, w_out, ln_g, ln_b,
                  tm=512, tq=1024, tqs=512):
    t, d = h.shape
    nh, dq, dv = MLA_HEADS, MLA_NOPE + MLA_ROPE, MLA_V
    half = MLA_ROPE // 2
    inv_freq = 1.0 / (ROPE_THETA ** (jnp.arange(half, dtype=F32) * (2.0 / MLA_ROPE)))
    nlat = w_in.shape[1]
    nt = seq // tm
    dp = MLA_QK_PAD
    qT, kc, vT = pl.pallas_call(
        _mla_proj_kernel,
        grid=(bsz, nt),
        in_specs=[pl.BlockSpec((tm, d), lambda b, i: (b * nt + i, 0)),
                  pl.BlockSpec((1, 1, tm), lambda b, i: (b, 0, i)),
                  _const_spec((nlat, d)), _const_spec((MLA_Q_LORA, 1)), _const_spec((MLA_KV_LORA, 1)),
                  _const_spec((nh * dq, MLA_Q_LORA)), _const_spec((nh * (MLA_NOPE + dv), MLA_KV_LORA)),
                  _const_spec((half, 1))],
        out_specs=[pl.BlockSpec((1, nh, dp, tm), lambda b, i: (b, 0, 0, i)),
                   pl.BlockSpec((1, nh, 1, tm, dp), lambda b, i: (b, 0, i, 0, 0)),
                   pl.BlockSpec((1, nh, 1, dv + 16, tm), lambda b, i: (b, 0, i, 0, 0))],
        out_shape=[jax.ShapeDtypeStruct((bsz, nh, dp, seq), BF16),
                   jax.ShapeDtypeStruct((bsz, nh, nt, tm, dp), BF16),
                   jax.ShapeDtypeStruct((bsz, nh, nt, dv + 16, tm), BF16)],
        compiler_params=_params("parallel", "parallel"),
        name="mla_proj",
    )(h, positions.reshape(bsz, 1, seq), w_in.T.astype(BF16), q_norm_g.astype(F32).reshape(-1, 1),
      kv_norm_g.astype(F32).reshape(-1, 1), w_q_b.T.astype(BF16), w_kv_b.T.astype(BF16),
      inv_freq.reshape(half, 1))

    tk, nkv = tm, nt
    tq = min(tq, seq)
    oT = pl.pallas_call(
        functools.partial(_attn_kernel, nkv=nkv, tqs=tqs),
        grid=(bsz, nh, seq // tq),
        in_specs=[pl.BlockSpec((1, 1, dp, tq), lambda b, hh, i: (b, hh, 0, i)),
                  pl.BlockSpec((1, 1, nkv, tk, dp), lambda b, hh, i: (b, hh, 0, 0, 0)),
                  pl.BlockSpec((1, 1, nkv, dv + 16, tk), lambda b, hh, i: (b, hh, 0, 0, 0))],
        out_specs=pl.BlockSpec((1, 1, dv, tq), lambda b, hh, i: (b, hh, 0, i)),
        out_shape=jax.ShapeDtypeStruct((bsz, nh, dv, seq), BF16),
        scratch_shapes=[pltpu.VMEM((tk, tq), F32), pltpu.VMEM((tk, tq), F32),
                        pltpu.VMEM((tk, tq), BF16), pltpu.VMEM((tk, tq), BF16),
                        pltpu.VMEM((1, tq), F32), pltpu.VMEM((1, tq), F32),
                        pltpu.VMEM((1, tq), F32), pltpu.VMEM((1, tq), F32),
                        pltpu.VMEM((1, tq), F32), pltpu.VMEM((dv + 16, tq), F32)],
        compiler_params=_params("parallel", "parallel", "arbitrary"),
        name="mla_attn",
    )(qT, kc, vT)

    return pl.pallas_call(
        _mla_out_kernel,
        grid=(bsz, nt),
        in_specs=[pl.BlockSpec((1, nh, dv, tm), lambda b, i: (b, 0, 0, i)),
                  pl.BlockSpec((tm, d), lambda b, i: (b * nt + i, 0)),
                  _const_spec((nh * dv, d)), _const_spec((1, d)), _const_spec((1, d))],
        out_specs=pl.BlockSpec((tm, d), lambda b, i: (b * nt + i, 0)),
        out_shape=jax.ShapeDtypeStruct((t, d), F32),
        compiler_params=_params("parallel", "parallel"),
        name="mla_out",
    )(oT, h, w_out.astype(BF16), ln_g.reshape(1, d), ln_b.reshape(1, d))


def kernel(x, positions, ln_mix_g, ln_mix_b, ln_ffn_g, ln_ffn_b, ffn_w_in, ffn_w_out, s5_w_in, s5_lam_re, s5_lam_im, s5_log_step, s5_b_re, s5_b_im, s5_c_re, s5_c_im, s5_d, s5_w_glu, s5_b_glu, s5_w_out, hg_w_in, hg_lower_bound, hg_norm_g, hg_w_out, mla_w_in, mla_q_norm_g, mla_w_q_b, mla_kv_norm_g, mla_w_kv_b, mla_w_out):
    bsz, seq, d = x.shape
    lbs = jax.nn.softmax(hg_lower_bound.astype(F32), axis=0)
    lbs = jnp.cumsum(lbs, axis=0) - lbs[0]
    h = x.reshape(bsz * seq, d)
    for layer in range(DEPTH):
        kind = layer % N_MIXERS
        slot = layer // N_MIXERS
        lg, lb = ln_mix_g[layer], ln_mix_b[layer]
        if kind == 0:
            h = _s5_mixer_ln(h, bsz, seq, s5_w_in[slot], s5_lam_re[slot], s5_lam_im[slot], s5_log_step[slot],
                             s5_b_re[slot], s5_b_im[slot], s5_c_re[slot], s5_c_im[slot], s5_d[slot],
                             s5_w_glu[slot], s5_b_glu[slot], s5_w_out[slot], lg, lb)
        elif kind == 1:
            h = _hg_mixer_ln(h, bsz, seq, hg_w_in[slot], lbs[layer], hg_norm_g[slot], hg_w_out[slot], lg, lb)
        else:
            h = _mla_mixer_ln(h, positions, bsz, seq, mla_w_in[slot], mla_q_norm_g[slot], mla_w_q_b[slot],
                              mla_kv_norm_g[slot], mla_w_kv_b[slot], mla_w_out[slot], lg, lb)
        h = _ffn_ln(h, ffn_w_in[layer], ffn_w_out[layer], ln_ffn_g[layer], ln_ffn_b[layer])
    return h.reshape(bsz, seq, d)
```

```python
import functools
import math

import jax
import jax.numpy as jnp
from jax import lax
from jax.experimental import pallas as pl
from jax.experimental.pallas import tpu as pltpu

F32 = jnp.float32
BF16 = jnp.bfloat16

DEPTH = 4
N_MIXERS = 3
S5_GROUP = 16
S5_STATE = 64
S5_CHUNK = 64
S5_PW_ROWS = 72
HG_HEAD_DIM = 128
HG_CHUNK = 128
HG_BLOCK = 512
MLA_HEADS = 8
MLA_NOPE = 128
MLA_ROPE = 64
MLA_V = 128
MLA_QK_PAD = 256
MLA_Q_LORA = 384
MLA_KV_LORA = 256
ROPE_THETA = 10000.0
DEEPNORM_ALPHA = (2 * DEPTH) ** 0.25
LN_EPS = 1e-5
RMS_EPS = 1e-6
FFN_CHUNK = 512
VMEM_LIMIT = 56 * 1024 * 1024


def _params(*sem):
    return pltpu.CompilerParams(dimension_semantics=sem, vmem_limit_bytes=VMEM_LIMIT)


def _dot(a, b):
    return jnp.dot(a, b, preferred_element_type=F32)


def _dot_nt(a, b):
    return lax.dot_general(a, b, (((1,), (1,)), ((), ())), preferred_element_type=F32)


def _dot_tn(a, b):
    return lax.dot_general(a, b, (((0,), (0,)), ((), ())), preferred_element_type=F32)


def _layer_norm(y, g, b):
    mu = jnp.mean(y, axis=-1, keepdims=True)
    d = y - mu
    var = jnp.mean(d * d, axis=-1, keepdims=True)
    return d * lax.rsqrt(var + LN_EPS) * g + b


def _const_spec(shape):
    return pl.BlockSpec(shape, lambda *_: (0,) * len(shape))


def _matmul_kernel(x_ref, w_ref, o_ref):
    o_ref[...] = _dot(x_ref[...].astype(BF16), w_ref[...]).astype(o_ref.dtype)


def _matmul(x, w, tm, out_dtype=F32):
    t, k = x.shape
    n = w.shape[1]
    return pl.pallas_call(
        _matmul_kernel,
        grid=(t // tm,),
        in_specs=[pl.BlockSpec((tm, k), lambda i: (i, 0)), _const_spec((k, n))],
        out_specs=pl.BlockSpec((tm, n), lambda i: (i, 0)),
        out_shape=jax.ShapeDtypeStruct((t, n), out_dtype),
        compiler_params=_params("parallel"),
        name="proj",
    )(x, w)


def _ffn_kernel(x_ref, win_ref, wout_ref, g_ref, b_ref, o_ref, *, hid):
    x = x_ref[...]
    xb = x.astype(BF16)
    acc = DEEPNORM_ALPHA * x
    for lo in range(0, hid, FFN_CHUNK):
        w = min(FFN_CHUNK, hid - lo)
        gate = _dot(xb, win_ref[:, lo:lo + w])
        up = _dot(xb, win_ref[:, hid + lo:hid + lo + w])
        mid = (gate * jax.nn.sigmoid(gate) * up).astype(BF16)
        acc = acc + _dot(mid, wout_ref[lo:lo + w, :])
    o_ref[...] = _layer_norm(acc, g_ref[...], b_ref[...])


def _ffn_ln(x, w_in, w_out, g, b, tm=512):
    t, d = x.shape
    hid = w_out.shape[0]
    return pl.pallas_call(
        functools.partial(_ffn_kernel, hid=hid),
        grid=(t // tm,),
        in_specs=[pl.BlockSpec((tm, d), lambda i: (i, 0)),
                  pl.BlockSpec((d, 2 * hid), lambda i: (0, 0), pipeline_mode=pl.Buffered(1)),
                  pl.BlockSpec((hid, d), lambda i: (0, 0), pipeline_mode=pl.Buffered(1)),
                  _const_spec((1, d)), _const_spec((1, d))],
        out_specs=pl.BlockSpec((tm, d), lambda i: (i, 0)),
        out_shape=jax.ShapeDtypeStruct((t, d), F32),
        compiler_params=_params("parallel"),
        name="ffn_ln",
    )(x, w_in.astype(BF16), w_out.astype(BF16), g.reshape(1, d), b.reshape(1, d))


def _s5_tables(lam_re, lam_im, log_step, b_re, b_im, c_re, c_im, nlev):
    L = S5_CHUNK
    lam = lax.complex(lam_re.astype(F32), lam_im.astype(F32))
    step = jnp.exp(log_step.astype(F32))[..., None]
    lam_dt = lam * step
    lam_bar = jnp.exp(lam_dt)
    b_bar = ((lam_bar - 1.0) / lam)[..., None] * lax.complex(b_re.astype(F32), b_im.astype(F32))
    g, p = lam.shape[1], lam.shape[2]
    taus = jnp.arange(S5_PW_ROWS, dtype=F32)
    pw = jnp.exp(lam_dt[:, None] * taus[None, :, None, None])
    pr, pi = jnp.real(pw), jnp.imag(pw)
    cat = lambda a, b: jnp.concatenate([a, b], axis=-1)
    pw_tab = jnp.stack([cat(pr, pr), cat(pi, pi)], axis=1)
    pw_tab = pw_tab.transpose(3, 0, 1, 2, 4).reshape(g, 4, S5_PW_ROWS, 2 * p)
    bt = jnp.swapaxes(b_bar, 2, 3)
    bre, bim = jnp.real(bt), jnp.imag(bt)
    cre, cim = c_re.astype(F32), c_im.astype(F32)
    c2 = cat(cre, -cim)
    bc_tab = jnp.stack([cat(bre, bim), cat(-bim, bre), c2, cat(-cim, -cre)], axis=2)
    bc_tab = bc_tab.transpose(1, 0, 2, 3, 4).reshape(g, 8, S5_GROUP, 2 * p)
    cmat = cat(c2[0], c2[1]).astype(BF16)
    k2 = (L * 2.0 ** jnp.arange(nlev, dtype=F32))
    a = jnp.exp(lam_dt[:, None] * k2[None, :, None, None])
    ar, ai = jnp.real(a), jnp.imag(a)
    p1 = cat(ar, ar).transpose(2, 0, 3, 1).reshape(g, 4 * p, nlev)
    p2 = cat(-ai, ai).transpose(2, 0, 3, 1).reshape(g, 4 * p, nlev)
    return pw_tab, bc_tab, cmat, p1, p2


def _s5_core_kernel(u_ref, pw_ref, bc_ref, cm_ref, p1_ref, p2_ref, y_ref, ut_sc, e_sc, gt_sc, bs_sc, tt_sc,
                    *, nc, nlev):
    L, hh, p, two_p = S5_CHUNK, S5_GROUP, S5_STATE, 2 * S5_STATE
    lh = L * hh
    n = ut_sc.shape[1]

    def blk(d, kind, tau):
        base = 4 * d + 2 * kind
        return (bc_ref[0, base] * pw_ref[0, 2 * d, tau:tau + 1, :]
                + bc_ref[0, base + 1] * pw_ref[0, 2 * d + 1, tau:tau + 1, :])

    zero = jnp.zeros((hh, two_p), BF16)
    for i in range(L):
        rows = slice(i * hh, (i + 1) * hh)
        ut_sc[rows, :] = u_ref[:, i * n:(i + 1) * n]
        e_sc[rows, :two_p] = blk(0, 0, L - 1 - i).astype(BF16)
        e_sc[rows, two_p:] = blk(1, 0, i).astype(BF16)
        gt_sc[rows, :two_p] = blk(0, 1, i + 1).astype(BF16)
        gt_sc[rows, two_p:] = blk(1, 1, L - i).astype(BF16)
    for rho in range(2 * L):
        rows = slice(rho * hh, (rho + 1) * hh)
        lag = L - 1 - rho
        bs_sc[rows, :two_p] = blk(0, 0, lag).astype(BF16) if 0 <= lag < L else zero
        bs_sc[rows, two_p:] = blk(1, 0, -lag).astype(BF16) if 0 <= -lag < L else zero
    strip = _dot_nt(cm_ref[0], bs_sc[...])
    for r in range(128 // hh):
        rot = strip if r == 0 else pltpu.roll(strip, 2 * lh - r * hh, 1)
        for j in range(L):
            a, rr = divmod(L - 1 - j, 128 // hh)
            if rr == r:
                tt_sc[j * hh:(j + 1) * hh, :] = rot[:, 128 * a:128 * a + lh].astype(BF16)

    ut = ut_sc[...]
    y = _dot(tt_sc[...], ut)
    s = _dot_tn(e_sc[...], ut)
    cidx = lax.broadcasted_iota(jnp.int32, (two_p, n), 1) & (nc - 1)

    def cmul(x, k, d):
        a1 = p1_ref[0, d * two_p:(d + 1) * two_p, k:k + 1]
        a2 = p2_ref[0, d * two_p:(d + 1) * two_p, k:k + 1]
        return x * a1 + jnp.concatenate([x[p:], x[:p]], axis=0) * a2

    xf, xb = s[:two_p], s[two_p:]
    for k in range(nlev):
        sh = 1 << k
        xf = xf + jnp.where(cidx >= sh, cmul(pltpu.roll(xf, sh, 1), k, 0), 0.0)
        xb = xb + jnp.where(cidx < nc - sh, cmul(pltpu.roll(xb, n - sh, 1), k, 1), 0.0)
    hf = jnp.where(cidx >= 1, pltpu.roll(xf, 1, 1), 0.0)
    hb = jnp.where(cidx < nc - 1, pltpu.roll(xb, n - 1, 1), 0.0)
    y = y + _dot(gt_sc[...], jnp.concatenate([hf, hb], axis=0).astype(BF16))
    for j in range(L):
        y_ref[:, j * n:(j + 1) * n] = y[j * hh:(j + 1) * hh, :]


def _s5_core(ut, tables, n, nc, nlev):
    d, t = ut.shape
    hh, two_p = S5_GROUP, 2 * S5_STATE
    lh = S5_CHUNK * hh
    blk = lambda a: pl.BlockSpec((1,) + a.shape[1:], lambda i: (i,) + (0,) * (a.ndim - 1))
    return pl.pallas_call(
        functools.partial(_s5_core_kernel, nc=nc, nlev=nlev),
        grid=(d // hh,),
        in_specs=[pl.BlockSpec((hh, t), lambda i: (i, 0))] + [blk(a) for a in tables],
        out_specs=pl.BlockSpec((hh, t), lambda i: (i, 0)),
        out_shape=jax.ShapeDtypeStruct((d, t), F32),
        scratch_shapes=[pltpu.VMEM((lh, n), BF16),
                        pltpu.VMEM((lh, 2 * two_p), BF16), pltpu.VMEM((lh, 2 * two_p), BF16),
                        pltpu.VMEM((2 * lh, 2 * two_p), BF16), pltpu.VMEM((lh, lh), BF16)],
        compiler_params=_params("parallel"),
        name="s5_core",
    )(ut, *tables)


def _s5_in_kernel(x_ref, wT_ref, o_ref):
    o_ref[...] = _dot_nt(wT_ref[...], x_ref[...].astype(BF16)).astype(o_ref.dtype)


def _s5_post_kernel(yT_ref, x_ref, winT_ref, d_ref, wgT_ref, bg_ref, wo_ref, g_ref, b_ref, o_ref):
    x = x_ref[...]
    u = _dot_nt(winT_ref[...], x.astype(BF16))
    y = yT_ref[...] + d_ref[...] * u
    y = jax.nn.gelu(y)
    z = _dot(wgT_ref[...], y.astype(BF16)) + bg_ref[...]
    y = y * jax.nn.sigmoid(z)
    m = _dot_tn(y.astype(BF16), wo_ref[...])
    o_ref[...] = _layer_norm(DEEPNORM_ALPHA * x + m, g_ref[...], b_ref[...])


def _s5_mixer_ln(h, bsz, seq, w_in, lam_re, lam_im, log_step, b_re, b_im, c_re, c_im,
                 d_skip, w_glu, b_glu, w_out, ln_g, ln_b, tm=512):
    t, d = h.shape
    L = S5_CHUNK
    nc = seq // L
    n = bsz * nc
    nlev = max(1, (nc - 1).bit_length())
    assert nc & (nc - 1) == 0
    hp = h.reshape(n, L, d).transpose(1, 0, 2).reshape(t, d)
    w_in_t = w_in.T.astype(BF16)
    tile = pl.BlockSpec((tm, d), lambda i: (i, 0))
    tile_t = pl.BlockSpec((d, tm), lambda i: (0, i))
    ut = pl.pallas_call(
        _s5_in_kernel,
        grid=(t // tm,),
        in_specs=[tile, _const_spec((d, d))],
        out_specs=tile_t,
        out_shape=jax.ShapeDtypeStruct((d, t), BF16),
        compiler_params=_params("parallel"),
        name="s5_in",
    )(hp, w_in_t)
    tables = _s5_tables(lam_re, lam_im, log_step, b_re, b_im, c_re, c_im, nlev)
    yt = _s5_core(ut, tables, n, nc, nlev)
    row = lambda v: v.reshape(1, d)
    col = lambda v: v.astype(F32).reshape(d, 1)
    out = pl.pallas_call(
        _s5_post_kernel,
        grid=(t // tm,),
        in_specs=[tile_t, tile, _const_spec((d, d)), _const_spec((d, 1)), _const_spec((d, d)), _const_spec((d, 1)),
                  _const_spec((d, d)), _const_spec((1, d)), _const_spec((1, d))],
        out_specs=tile,
        out_shape=jax.ShapeDtypeStruct((t, d), F32),
        compiler_params=_params("parallel"),
        name="s5_post",
    )(yt, hp, w_in_t, col(d_skip), w_glu.T.astype(BF16), col(b_glu), w_out.astype(BF16), row(ln_g), row(ln_b))
    return out.reshape(L, n, d).transpose(1, 0, 2).reshape(t, d)


def _hg_masks(c):
    nlev = c.bit_length() - 1
    t = jnp.arange(c)[:, None]
    s = jnp.arange(c)[None, :]
    ms = []
    for lev in range(nlev):
        w = 1 << lev
        ms.append(((t & w) != 0) & ((s & w) == 0) & ((t >> (lev + 1)) == (s >> (lev + 1))))
    ms.append(t == s)
    fw = jnp.stack(ms).astype(F32)
    return jnp.stack([fw, fw.transpose(0, 2, 1)])


def _hg_gate(z, la, l1m, onem):
    ls = jnp.minimum(z, 0.0) - jnp.log1p(jnp.exp(-jnp.abs(z)))
    b = l1m + ls
    lf = jnp.maximum(la, b) + jnp.log1p(jnp.exp(-jnp.abs(la - b)))
    return lf, onem * jax.nn.sigmoid(-z)


def _hg_chunk(q, z, v, la, l1m, onem, mask_ref, d, st_ref):
    c = q.shape[0]
    nlev = c.bit_length() - 1
    q = q * jax.nn.sigmoid(q)
    lf, k = _hg_gate(z, la, l1m, onem)
    lf = lf * math.log2(math.e)
    qb, kb, vb = q.astype(BF16), k.astype(BF16), v.astype(BF16)
    row = lax.broadcasted_iota(jnp.int32, lf.shape, 0)
    scores = mask_ref[d, nlev] * _dot_nt(qb, kb)
    p, tot = (lf if d == 0 else jnp.zeros_like(lf)), lf
    for lev in range(nlev):
        w = 1 << lev
        odd = (row & w) != 0
        f = jnp.exp2(jnp.where(odd, p, tot - p)).astype(BF16)
        scores = scores + mask_ref[d, lev] * _dot_nt(qb * f, kb * f)
        up = pltpu.roll(tot, w, 0)
        dn = pltpu.roll(tot, c - w, 0)
        p = p + jnp.where(odd, up, 0.0)
        tot = tot + jnp.where(odd, up, dn)
    eq, ek = (p, tot - p) if d == 0 else (tot - p, p)
    st = st_ref[...]
    o = _dot(scores.astype(BF16), vb)
    o = o + _dot_nt(qb * jnp.exp2(eq).astype(BF16), st.astype(BF16))
    st_ref[...] = st * jnp.exp2(tot[0:1, :]) + _dot_tn(vb, kb * jnp.exp2(ek).astype(BF16))
    return o


def _hg_rec_kernel(qf_ref, zf_ref, vf_ref, qb_ref, zb_ref, vb_ref, la_ref, l1m_ref, onem_ref, mask_ref,
                   of_ref, ob_ref, stf_ref, stb_ref, *, nsub):
    @pl.when(pl.program_id(2) == 0)
    def _():
        stf_ref[...] = jnp.zeros_like(stf_ref)
        stb_ref[...] = jnp.zeros_like(stb_ref)

    la, l1m, onem = la_ref[...], l1m_ref[...], onem_ref[...]
    c = HG_CHUNK

    def body(j, carry):
        lo = pl.multiple_of(j * c, c)
        hi = pl.multiple_of((nsub - 1 - j) * c, c)
        rf = pl.ds(lo, c)
        rb = pl.ds(hi, c)
        of_ref[rf, :] = _hg_chunk(qf_ref[rf, :], zf_ref[rf, :], vf_ref[rf, :], la, l1m, onem, mask_ref, 0, stf_ref)
        ob_ref[rb, :] = _hg_chunk(qb_ref[rb, :], zb_ref[rb, :], vb_ref[rb, :], la, l1m, onem, mask_ref, 1, stb_ref)
        return carry

    lax.fori_loop(0, nsub, body, 0, unroll=True)


def _hg_post_kernel(of_ref, ob_ref, g_ref, h_ref, ng_ref, wo_ref, lg_ref, lb_ref, o_ref):
    o = of_ref[...] + ob_ref[...]
    hd = HG_HEAD_DIM
    parts = []
    for i in range(o.shape[1] // hd):
        oh = o[:, i * hd:(i + 1) * hd]
        parts.append(oh * lax.rsqrt(jnp.mean(oh * oh, axis=-1, keepdims=True) + RMS_EPS))
    on = jnp.concatenate(parts, axis=1) * ng_ref[...]
    gate = g_ref[...]
    y = on * (gate * jax.nn.sigmoid(gate))
    m = _dot(y.astype(BF16), wo_ref[...])
    o_ref[...] = _layer_norm(DEEPNORM_ALPHA * h_ref[...] + m, lg_ref[...], lb_ref[...])


def _hg_mixer_ln(h, bsz, seq, w_in, lb, norm_g, w_out, ln_g, ln_b, tm=512):
    t, d = h.shape
    hd = HG_HEAD_DIM
    nh = d // hd
    blk = min(HG_BLOCK, seq)
    nblk = seq // blk
    proj = _matmul(h, w_in.astype(BF16), tm)
    lb = lb.astype(F32).reshape(1, d)
    la, l1m, onem = jnp.log(lb), jnp.log1p(-lb), 1.0 - lb
    masks = _hg_masks(HG_CHUNK)

    def fw(col):
        return pl.BlockSpec((blk, hd), lambda b, hh, i: (b * nblk + i, col * nh + hh))

    def bw(col):
        return pl.BlockSpec((blk, hd), lambda b, hh, i: (b * nblk + nblk - 1 - i, col * nh + hh))

    vec = pl.BlockSpec((1, hd), lambda b, hh, i: (0, hh))
    o_fw, o_bw = pl.pallas_call(
        functools.partial(_hg_rec_kernel, nsub=blk // HG_CHUNK),
        grid=(bsz, nh, nblk),
        in_specs=[fw(0), fw(1), fw(3), bw(0), bw(2), bw(3), vec, vec, vec, _const_spec(masks.shape)],
        out_specs=[pl.BlockSpec((blk, hd), lambda b, hh, i: (b * nblk + i, hh)),
                   pl.BlockSpec((blk, hd), lambda b, hh, i: (b * nblk + nblk - 1 - i, hh))],
        out_shape=[jax.ShapeDtypeStruct((t, d), F32)] * 2,
        scratch_shapes=[pltpu.VMEM((hd, hd), F32), pltpu.VMEM((hd, hd), F32)],
        compiler_params=_params("parallel", "parallel", "arbitrary"),
        name="hg_rec",
    )(proj, proj, proj, proj, proj, proj, la, l1m, onem, masks)

    row = lambda v: v.reshape(1, d)
    tile = pl.BlockSpec((tm, d), lambda i: (i, 0))
    return pl.pallas_call(
        _hg_post_kernel,
        grid=(t // tm,),
        in_specs=[tile, tile, pl.BlockSpec((tm, d), lambda i: (i, 4)), tile, _const_spec((1, d)),
                  _const_spec((d, d)), _const_spec((1, d)), _const_spec((1, d))],
        out_specs=tile,
        out_shape=jax.ShapeDtypeStruct((t, d), F32),
        compiler_params=_params("parallel"),
        name="hg_post",
    )(o_fw, o_bw, proj, h, jnp.tile(norm_g.astype(F32), nh).reshape(1, d), w_out.astype(BF16), row(ln_g), row(ln_b))


def _mla_proj_kernel(x_ref, pos_ref, winT_ref, gq_ref, gkv_ref, wqT_ref, wkvT_ref, invf_ref,
                     qT_ref, k_ref, vT_ref):
    ql, kvl, half = MLA_Q_LORA, MLA_KV_LORA, MLA_ROPE // 2
    dq = MLA_NOPE + MLA_ROPE
    lat = _dot_nt(winT_ref[...], x_ref[...].astype(BF16))
    q_lat, kv_lat, k_rope = lat[:ql], lat[ql:ql + kvl], lat[ql + kvl:]

    def rms(v, g):
        return (v * lax.rsqrt(jnp.mean(v * v, axis=0, keepdims=True) + RMS_EPS) * g).astype(BF16)

    q = _dot(wqT_ref[...], rms(q_lat, gq_ref[...]))
    kv = _dot(wkvT_ref[...], rms(kv_lat, gkv_ref[...]))
    ang = invf_ref[...] * pos_ref[0].astype(F32)
    cos, sin = jnp.cos(ang), jnp.sin(ang)

    def rope(t1, t2):
        return t1 * cos - t2 * sin, t1 * sin + t2 * cos

    scale = dq ** -0.5 * math.log2(math.e)
    ones_rows = (lax.broadcasted_iota(jnp.int32, (16, lat.shape[1]), 0) == 0).astype(BF16)
    k1, k2 = rope(k_rope[:half], k_rope[half:])
    pad = jnp.zeros((MLA_QK_PAD - dq, lat.shape[1]), F32)
    k_tail = jnp.concatenate([k1, k2, pad], axis=0).T.astype(BF16)
    for hh in range(MLA_HEADS):
        qh = q[hh * dq:(hh + 1) * dq]
        q1, q2 = rope(qh[MLA_NOPE:MLA_NOPE + half], qh[MLA_NOPE + half:])
        qT_ref[0, hh, 0:MLA_NOPE, :] = (qh[:MLA_NOPE] * scale).astype(BF16)
        qT_ref[0, hh, MLA_NOPE:MLA_NOPE + half, :] = (q1 * scale).astype(BF16)
        qT_ref[0, hh, MLA_NOPE + half:dq, :] = (q2 * scale).astype(BF16)
        qT_ref[0, hh, dq:, :] = pad.astype(BF16)
        kvh = kv[hh * (MLA_NOPE + MLA_V):(hh + 1) * (MLA_NOPE + MLA_V)]
        k_ref[0, hh, 0, :, 0:MLA_NOPE] = kvh[:MLA_NOPE].T.astype(BF16)
        k_ref[0, hh, 0, :, MLA_NOPE:] = k_tail
        vT_ref[0, hh, 0, 0:MLA_V, :] = kvh[MLA_NOPE:].astype(BF16)
        vT_ref[0, hh, 0, MLA_V:, :] = ones_rows


def _attn_kernel(qT_ref, k_ref, vT_ref, o_ref, s0_sc, s1_sc, t0_sc, t1_sc, m_sc, acc_sc, *, nkv, tqs, tks):
    dv = MLA_V
    s_sc, t_sc = (s0_sc, s1_sc), (t0_sc, t1_sc)
    tk, tq = s0_sc.shape
    m_sc[...] = jnp.full_like(m_sc, -jnp.inf)
    acc_sc[...] = jnp.zeros_like(acc_sc)

    def scores(j, par):
        for c in range(tq // tqs):
            cols = slice(c * tqs, (c + 1) * tqs)
            s = _dot(k_ref[0, 0, j], qT_ref[0, 0, :, cols])
            s_sc[par][:, cols] = s
            t_sc[par][:, cols] = jnp.max(s, axis=0, keepdims=True)

    def softmax_pv(j, par):
        for c in range(tq // tqs):
            cols = slice(c * tqs, (c + 1) * tqs)
            m_prev = m_sc[:, cols]
            m_new = jnp.maximum(m_prev, t_sc[par][:, cols])
            acc = jnp.exp2(m_prev - m_new) * acc_sc[:, cols]
            for r in range(tk // tks):
                rows = slice(r * tks, (r + 1) * tks)
                p = jnp.exp2(s_sc[par][rows, cols] - m_new).astype(BF16)
                acc = acc + _dot(vT_ref[0, 0, j, :, rows], p)
            acc_sc[:, cols] = acc
            m_sc[:, cols] = m_new

    scores(0, 0)

    def body(jj, carry):
        j = 2 * jj
        scores(j + 1, 1)
        softmax_pv(j, 0)
        scores(j + 2, 0)
        softmax_pv(j + 1, 1)
        return carry

    lax.fori_loop(0, nkv // 2 - 1, body, 0)
    scores(nkv - 1, 1)
    softmax_pv(nkv - 2, 0)
    softmax_pv(nkv - 1, 1)
    acc = acc_sc[...]
    o_ref[0, 0] = (acc[:dv] / acc[dv:dv + 1]).astype(o_ref.dtype)


def _mla_out_kernel(oT_ref, h_ref, wo_ref, g_ref, b_ref, o_ref):
    oT = oT_ref[0]
    oT = oT.reshape(oT.shape[0] * oT.shape[1], oT.shape[2])
    m = _dot_tn(oT, wo_ref[...])
    o_ref[...] = _layer_norm(DEEPNORM_ALPHA * h_ref[...] + m, g_ref[...], b_ref[...])


def _mla_mixer_ln(h, positions, bsz, seq, w_in, q_norm_g, w_q_b, kv_norm_g, w_kv_b, w_out, ln_g, ln_b,
                  tm=512, tq=1024, tqs=256, tks=256):
    t, d = h.shape
    nh, dq, dv = MLA_HEADS, MLA_NOPE + MLA_ROPE, MLA_V
    half = MLA_ROPE // 2
    inv_freq = 1.0 / (ROPE_THETA ** (jnp.arange(half, dtype=F32) * (2.0 / MLA_ROPE)))
    nlat = w_in.shape[1]
    nt = seq // tm
    dp = MLA_QK_PAD
    qT, kc, vT = pl.pallas_call(
        _mla_proj_kernel,
        grid=(bsz, nt),
        in_specs=[pl.BlockSpec((tm, d), lambda b, i: (b * nt + i, 0)),
                  pl.BlockSpec((1, 1, tm), lambda b, i: (b, 0, i)),
                  _const_spec((nlat, d)), _const_spec((MLA_Q_LORA, 1)), _const_spec((MLA_KV_LORA, 1)),
                  _const_spec((nh * dq, MLA_Q_LORA)), _const_spec((nh * (MLA_NOPE + dv), MLA_KV_LORA)),
                  _const_spec((half, 1))],
        out_specs=[pl.BlockSpec((1, nh, dp, tm), lambda b, i: (b, 0, 0, i)),
                   pl.BlockSpec((1, nh, 1, tm, dp), lambda b, i: (b, 0, i, 0, 0)),
                   pl.BlockSpec((1, nh, 1, dv + 16, tm), lambda b, i: (b, 0, i, 0, 0))],
        out_shape=[jax.ShapeDtypeStruct((bsz, nh, dp, seq), BF16),
                   jax.ShapeDtypeStruct((bsz, nh, nt, tm, dp), BF16),
                   jax.ShapeDtypeStruct((bsz, nh, nt, dv + 16, tm), BF16)],
        compiler_params=_params("parallel", "parallel"),
        name="mla_proj",
    )(h, positions.reshape(bsz, 1, seq), w_in.T.astype(BF16), q_norm_g.astype(F32).reshape(-1, 1),
      kv_norm_g.astype(F32).reshape(-1, 1), w_q_b.T.astype(BF16), w_kv_b.T.astype(BF16),
      inv_freq.reshape(half, 1))

    tk, nkv = tm, nt
    tq = min(tq, seq)
    oT = pl.pallas_call(
        functools.partial(_attn_kernel, nkv=nkv, tqs=tqs, tks=tks),
        grid=(bsz, nh, seq // tq),
        in_specs=[pl.BlockSpec((1, 1, dp, tq), lambda b, hh, i: (b, hh, 0, i)),
                  pl.BlockSpec((1, 1, nkv, tk, dp), lambda b, hh, i: (b, hh, 0, 0, 0)),
                  pl.BlockSpec((1, 1, nkv, dv + 16, tk), lambda b, hh, i: (b, hh, 0, 0, 0))],
        out_specs=pl.BlockSpec((1, 1, dv, tq), lambda b, hh, i: (b, hh, 0, i)),
        out_shape=jax.ShapeDtypeStruct((bsz, nh, dv, seq), BF16),
        scratch_shapes=[pltpu.VMEM((tk, tq), F32), pltpu.VMEM((tk, tq), F32),
                        pltpu.VMEM((1, tq), F32), pltpu.VMEM((1, tq), F32),
                        pltpu.VMEM((1, tq), F32), pltpu.VMEM((dv + 16, tq), F32)],
        compiler_params=_params("parallel", "parallel", "arbitrary"),
        name="mla_attn",
    )(qT, kc, vT)

    return pl.pallas_call(
        _mla_out_kernel,
        grid=(bsz, nt),
        in_specs=[pl.BlockSpec((1, nh, dv, tm), lambda b, i: (b, 0, 0, i)),
                  pl.BlockSpec((tm, d), lambda b, i: (b * nt + i, 0)),
                  _const_spec((nh * dv, d)), _const_spec((1, d)), _const_spec((1, d))],
        out_specs=pl.BlockSpec((tm, d), lambda b, i: (b * nt + i, 0)),
        out_shape=jax.ShapeDtypeStruct((t, d), F32),
        compiler_params=_params("parallel", "parallel"),
        name="mla_out",
    )(oT, h, w_out.astype(BF16), ln_g.reshape(1, d), ln_b.reshape(1, d))


def kernel(x, positions, ln_mix_g, ln_mix_b, ln_ffn_g, ln_ffn_b, ffn_w_in, ffn_w_out, s5_w_in, s5_lam_re, s5_lam_im, s5_log_step, s5_b_re, s5_b_im, s5_c_re, s5_c_im, s5_d, s5_w_glu, s5_b_glu, s5_w_out, hg_w_in, hg_lower_bound, hg_norm_g, hg_w_out, mla_w_in, mla_q_norm_g, mla_w_q_b, mla_kv_norm_g, mla_w_kv_b, mla_w_out):
    bsz, seq, d = x.shape
    lbs = jax.nn.softmax(hg_lower_bound.astype(F32), axis=0)
    lbs = jnp.cumsum(lbs, axis=0) - lbs[0]
    h = x.reshape(bsz * seq, d)
    for layer in range(DEPTH):
        kind = layer % N_MIXERS
        slot = layer // N_MIXERS
        lg, lb = ln_mix_g[layer], ln_mix_b[layer]
        if kind == 0:
            h = _s5_mixer_ln(h, bsz, seq, s5_w_in[slot], s5_lam_re[slot], s5_lam_im[slot], s5_log_step[slot],
                             s5_b_re[slot], s5_b_im[slot], s5_c_re[slot], s5_c_im[slot], s5_d[slot],
                             s5_w_glu[slot], s5_b_glu[slot], s5_w_out[slot], lg, lb)
        elif kind == 1:
            h = _hg_mixer_ln(h, bsz, seq, hg_w_in[slot], lbs[layer], hg_norm_g[slot], hg_w_out[slot], lg, lb)
        else:
            h = _mla_mixer_ln(h, positions, bsz, seq, mla_w_in[slot], mla_q_norm_g[slot], mla_w_q_b[slot],
                              mla_kv_norm_g[slot], mla_w_kv_b[slot], mla_w_out[slot], lg, lb)
        h = _ffn_ln(h, ffn_w_in[layer], ffn_w_out[layer], ln_ffn_g[layer], ln_ffn_b[layer])
    return h.reshape(bsz, seq, d)
```

```python
import functools
import math

import jax
import jax.numpy as jnp
from jax import lax
from jax.experimental import pallas as pl
from jax.experimental.pallas import tpu as pltpu

F32 = jnp.float32
BF16 = jnp.bfloat16

DEPTH = 4
N_MIXERS = 3
S5_GROUP = 16
S5_STATE = 64
S5_CHUNK = 64
S5_PW_ROWS = 72
HG_HEAD_DIM = 128
HG_CHUNK = 128
HG_BLOCK = 512
MLA_HEADS = 8
MLA_NOPE = 128
MLA_ROPE = 64
MLA_V = 128
MLA_QK_PAD = 256
MLA_Q_LORA = 384
MLA_KV_LORA = 256
ROPE_THETA = 10000.0
DEEPNORM_ALPHA = (2 * DEPTH) ** 0.25
LN_EPS = 1e-5
RMS_EPS = 1e-6
FFN_CHUNK = 512
VMEM_LIMIT = 56 * 1024 * 1024


def _params(*sem):
    return pltpu.CompilerParams(dimension_semantics=sem, vmem_limit_bytes=VMEM_LIMIT)


def _dot(a, b):
    return jnp.dot(a, b, preferred_element_type=F32)


def _dot_nt(a, b):
    return lax.dot_general(a, b, (((1,), (1,)), ((), ())), preferred_element_type=F32)


def _dot_tn(a, b):
    return lax.dot_general(a, b, (((0,), (0,)), ((), ())), preferred_element_type=F32)


def _layer_norm(y, g, b):
    mu = jnp.mean(y, axis=-1, keepdims=True)
    d = y - mu
    var = jnp.mean(d * d, axis=-1, keepdims=True)
    return d * lax.rsqrt(var + LN_EPS) * g + b


def _const_spec(shape):
    return pl.BlockSpec(shape, lambda *_: (0,) * len(shape))


def _matmul_kernel(x_ref, w_ref, o_ref):
    o_ref[...] = _dot(x_ref[...].astype(BF16), w_ref[...]).astype(o_ref.dtype)


def _matmul(x, w, tm, out_dtype=F32):
    t, k = x.shape
    n = w.shape[1]
    return pl.pallas_call(
        _matmul_kernel,
        grid=(t // tm,),
        in_specs=[pl.BlockSpec((tm, k), lambda i: (i, 0)), _const_spec((k, n))],
        out_specs=pl.BlockSpec((tm, n), lambda i: (i, 0)),
        out_shape=jax.ShapeDtypeStruct((t, n), out_dtype),
        compiler_params=_params("parallel"),
        name="proj",
    )(x, w)


def _ffn_kernel(x_ref, win_ref, wout_ref, g_ref, b_ref, o_ref, *, hid):
    x = x_ref[...]
    xb = x.astype(BF16)
    acc = DEEPNORM_ALPHA * x
    for lo in range(0, hid, FFN_CHUNK):
        w = min(FFN_CHUNK, hid - lo)
        gate = _dot(xb, win_ref[:, lo:lo + w])
        up = _dot(xb, win_ref[:, hid + lo:hid + lo + w])
        mid = (gate * jax.nn.sigmoid(gate) * up).astype(BF16)
        acc = acc + _dot(mid, wout_ref[lo:lo + w, :])
    o_ref[...] = _layer_norm(acc, g_ref[...], b_ref[...])


def _ffn_ln(x, w_in, w_out, g, b, tm=512):
    t, d = x.shape
    hid = w_out.shape[0]
    return pl.pallas_call(
        functools.partial(_ffn_kernel, hid=hid),
        grid=(t // tm,),
        in_specs=[pl.BlockSpec((tm, d), lambda i: (i, 0)),
                  pl.BlockSpec((d, 2 * hid), lambda i: (0, 0), pipeline_mode=pl.Buffered(1)),
                  pl.BlockSpec((hid, d), lambda i: (0, 0), pipeline_mode=pl.Buffered(1)),
                  _const_spec((1, d)), _const_spec((1, d))],
        out_specs=pl.BlockSpec((tm, d), lambda i: (i, 0)),
        out_shape=jax.ShapeDtypeStruct((t, d), F32),
        compiler_params=_params("parallel"),
        name="ffn_ln",
    )(x, w_in.astype(BF16), w_out.astype(BF16), g.reshape(1, d), b.reshape(1, d))


def _s5_tables(lam_re, lam_im, log_step, b_re, b_im, c_re, c_im, nlev):
    L = S5_CHUNK
    lam = lax.complex(lam_re.astype(F32), lam_im.astype(F32))
    step = jnp.exp(log_step.astype(F32))[..., None]
    lam_dt = lam * step
    lam_bar = jnp.exp(lam_dt)
    b_bar = ((lam_bar - 1.0) / lam)[..., None] * lax.complex(b_re.astype(F32), b_im.astype(F32))
    g, p = lam.shape[1], lam.shape[2]
    taus = jnp.arange(S5_PW_ROWS, dtype=F32)
    pw = jnp.exp(lam_dt[:, None] * taus[None, :, None, None])
    pr, pi = jnp.real(pw), jnp.imag(pw)
    cat = lambda a, b: jnp.concatenate([a, b], axis=-1)
    pw_tab = jnp.stack([cat(pr, pr), cat(pi, pi)], axis=1)
    pw_tab = pw_tab.transpose(3, 0, 1, 2, 4).reshape(g, 4, S5_PW_ROWS, 2 * p)
    bt = jnp.swapaxes(b_bar, 2, 3)
    bre, bim = jnp.real(bt), jnp.imag(bt)
    cre, cim = c_re.astype(F32), c_im.astype(F32)
    c2 = cat(cre, -cim)
    bc_tab = jnp.stack([cat(bre, bim), cat(-bim, bre), c2, cat(-cim, -cre)], axis=2)
    bc_tab = bc_tab.transpose(1, 0, 2, 3, 4).reshape(g, 8, S5_GROUP, 2 * p)
    cmat = cat(c2[0], c2[1]).astype(BF16)
    k2 = (L * 2.0 ** jnp.arange(nlev, dtype=F32))
    a = jnp.exp(lam_dt[:, None] * k2[None, :, None, None])
    ar, ai = jnp.real(a), jnp.imag(a)
    p1 = cat(ar, ar).transpose(2, 0, 3, 1).reshape(g, 4 * p, nlev)
    p2 = cat(-ai, ai).transpose(2, 0, 3, 1).reshape(g, 4 * p, nlev)
    return pw_tab, bc_tab, cmat, p1, p2


def _s5_core_kernel(u_ref, pw_ref, bc_ref, cm_ref, p1_ref, p2_ref, y_ref, ut_sc, e_sc, gt_sc, bs_sc, tt_sc,
                    *, nc, nlev):
    L, hh, p, two_p = S5_CHUNK, S5_GROUP, S5_STATE, 2 * S5_STATE
    lh = L * hh
    n = ut_sc.shape[1]

    def blk(d, kind, tau):
        base = 4 * d + 2 * kind
        return (bc_ref[0, base] * pw_ref[0, 2 * d, tau:tau + 1, :]
                + bc_ref[0, base + 1] * pw_ref[0, 2 * d + 1, tau:tau + 1, :])

    zero = jnp.zeros((hh, two_p), BF16)
    for i in range(L):
        rows = slice(i * hh, (i + 1) * hh)
        ut_sc[rows, :] = u_ref[:, i * n:(i + 1) * n]
        e_sc[rows, :two_p] = blk(0, 0, L - 1 - i).astype(BF16)
        e_sc[rows, two_p:] = blk(1, 0, i).astype(BF16)
        gt_sc[rows, :two_p] = blk(0, 1, i + 1).astype(BF16)
        gt_sc[rows, two_p:] = blk(1, 1, L - i).astype(BF16)
    for rho in range(2 * L):
        rows = slice(rho * hh, (rho + 1) * hh)
        lag = L - 1 - rho
        bs_sc[rows, :two_p] = blk(0, 0, lag).astype(BF16) if 0 <= lag < L else zero
        bs_sc[rows, two_p:] = blk(1, 0, -lag).astype(BF16) if 0 <= -lag < L else zero
    strip = _dot_nt(cm_ref[0], bs_sc[...])
    for r in range(128 // hh):
        rot = strip if r == 0 else pltpu.roll(strip, 2 * lh - r * hh, 1)
        for j in range(L):
            a, rr = divmod(L - 1 - j, 128 // hh)
            if rr == r:
                tt_sc[j * hh:(j + 1) * hh, :] = rot[:, 128 * a:128 * a + lh].astype(BF16)

    ut = ut_sc[...]
    y = _dot(tt_sc[...], ut)
    s = _dot_tn(e_sc[...], ut)
    cidx = lax.broadcasted_iota(jnp.int32, (two_p, n), 1) & (nc - 1)

    def cmul(x, k, d):
        a1 = p1_ref[0, d * two_p:(d + 1) * two_p, k:k + 1]
        a2 = p2_ref[0, d * two_p:(d + 1) * two_p, k:k + 1]
        return x * a1 + jnp.concatenate([x[p:], x[:p]], axis=0) * a2

    xf, xb = s[:two_p], s[two_p:]
    for k in range(nlev):
        sh = 1 << k
        xf = xf + jnp.where(cidx >= sh, cmul(pltpu.roll(xf, sh, 1), k, 0), 0.0)
        xb = xb + jnp.where(cidx < nc - sh, cmul(pltpu.roll(xb, n - sh, 1), k, 1), 0.0)
    hf = jnp.where(cidx >= 1, pltpu.roll(xf, 1, 1), 0.0)
    hb = jnp.where(cidx < nc - 1, pltpu.roll(xb, n - 1, 1), 0.0)
    y = y + _dot(gt_sc[...], jnp.concatenate([hf, hb], axis=0).astype(BF16))
    for j in range(L):
        y_ref[:, j * n:(j + 1) * n] = y[j * hh:(j + 1) * hh, :]


def _s5_core(ut, tables, n, nc, nlev):
    d, t = ut.shape
    hh, two_p = S5_GROUP, 2 * S5_STATE
    lh = S5_CHUNK * hh
    blk = lambda a: pl.BlockSpec((1,) + a.shape[1:], lambda i: (i,) + (0,) * (a.ndim - 1))
    return pl.pallas_call(
        functools.partial(_s5_core_kernel, nc=nc, nlev=nlev),
        grid=(d // hh,),
        in_specs=[pl.BlockSpec((hh, t), lambda i: (i, 0))] + [blk(a) for a in tables],
        out_specs=pl.BlockSpec((hh, t), lambda i: (i, 0)),
        out_shape=jax.ShapeDtypeStruct((d, t), F32),
        scratch_shapes=[pltpu.VMEM((lh, n), BF16),
                        pltpu.VMEM((lh, 2 * two_p), BF16), pltpu.VMEM((lh, 2 * two_p), BF16),
                        pltpu.VMEM((2 * lh, 2 * two_p), BF16), pltpu.VMEM((lh, lh), BF16)],
        compiler_params=_params("parallel"),
        name="s5_core",
    )(ut, *tables)


def _s5_in_kernel(x_ref, wT_ref, o_ref):
    o_ref[...] = _dot_nt(wT_ref[...], x_ref[...].astype(BF16)).astype(o_ref.dtype)


def _s5_post_kernel(yT_ref, x_ref, winT_ref, d_ref, wgT_ref, bg_ref, wo_ref, g_ref, b_ref, o_ref):
    x = x_ref[...]
    u = _dot_nt(winT_ref[...], x.astype(BF16))
    y = yT_ref[...] + d_ref[...] * u
    y = jax.nn.gelu(y)
    z = _dot(wgT_ref[...], y.astype(BF16)) + bg_ref[...]
    y = y * jax.nn.sigmoid(z)
    m = _dot_tn(y.astype(BF16), wo_ref[...])
    o_ref[...] = _layer_norm(DEEPNORM_ALPHA * x + m, g_ref[...], b_ref[...])


def _s5_mixer_ln(h, bsz, seq, w_in, lam_re, lam_im, log_step, b_re, b_im, c_re, c_im,
                 d_skip, w_glu, b_glu, w_out, ln_g, ln_b, tm=512):
    t, d = h.shape
    L = S5_CHUNK
    nc = seq // L
    n = bsz * nc
    nlev = max(1, (nc - 1).bit_length())
    assert nc & (nc - 1) == 0
    hp = h.reshape(n, L, d).transpose(1, 0, 2).reshape(t, d)
    w_in_t = w_in.T.astype(BF16)
    tile = pl.BlockSpec((tm, d), lambda i: (i, 0))
    tile_t = pl.BlockSpec((d, tm), lambda i: (0, i))
    ut = pl.pallas_call(
        _s5_in_kernel,
        grid=(t // tm,),
        in_specs=[tile, _const_spec((d, d))],
        out_specs=tile_t,
        out_shape=jax.ShapeDtypeStruct((d, t), BF16),
        compiler_params=_params("parallel"),
        name="s5_in",
    )(hp, w_in_t)
    tables = _s5_tables(lam_re, lam_im, log_step, b_re, b_im, c_re, c_im, nlev)
    yt = _s5_core(ut, tables, n, nc, nlev)
    row = lambda v: v.reshape(1, d)
    col = lambda v: v.astype(F32).reshape(d, 1)
    out = pl.pallas_call(
        _s5_post_kernel,
        grid=(t // tm,),
        in_specs=[tile_t, tile, _const_spec((d, d)), _const_spec((d, 1)), _const_spec((d, d)), _const_spec((d, 1)),
                  _const_spec((d, d)), _const_spec((1, d)), _const_spec((1, d))],
        out_specs=tile,
        out_shape=jax.ShapeDtypeStruct((t, d), F32),
        compiler_params=_params("parallel"),
        name="s5_post",
    )(yt, hp, w_in_t, col(d_skip), w_glu.T.astype(BF16), col(b_glu), w_out.astype(BF16), row(ln_g), row(ln_b))
    return out.reshape(L, n, d).transpose(1, 0, 2).reshape(t, d)


def _hg_masks(c):
    nlev = c.bit_length() - 1
    t = jnp.arange(c)[:, None]
    s = jnp.arange(c)[None, :]
    ms = []
    for lev in range(nlev):
        w = 1 << lev
        ms.append(((t & w) != 0) & ((s & w) == 0) & ((t >> (lev + 1)) == (s >> (lev + 1))))
    ms.append(t == s)
    fw = jnp.stack(ms).astype(F32)
    return jnp.stack([fw, fw.transpose(0, 2, 1)])


def _hg_gate(z, la, l1m, onem):
    ls = jnp.minimum(z, 0.0) - jnp.log1p(jnp.exp(-jnp.abs(z)))
    b = l1m + ls
    lf = jnp.maximum(la, b) + jnp.log1p(jnp.exp(-jnp.abs(la - b)))
    return lf, onem * jax.nn.sigmoid(-z)


def _hg_chunk(q, z, v, la, l1m, onem, mask_ref, d, st_ref):
    c = q.shape[0]
    nlev = c.bit_length() - 1
    q = q * jax.nn.sigmoid(q)
    lf, k = _hg_gate(z, la, l1m, onem)
    lf = lf * math.log2(math.e)
    qb, kb, vb = q.astype(BF16), k.astype(BF16), v.astype(BF16)
    row = lax.broadcasted_iota(jnp.int32, lf.shape, 0)
    scores = mask_ref[d, nlev] * _dot_nt(qb, kb)
    p, tot = (lf if d == 0 else jnp.zeros_like(lf)), lf
    for lev in range(nlev):
        w = 1 << lev
        odd = (row & w) != 0
        f = jnp.exp2(jnp.where(odd, p, tot - p)).astype(BF16)
        scores = scores + mask_ref[d, lev] * _dot_nt(qb * f, kb * f)
        up = pltpu.roll(tot, w, 0)
        dn = pltpu.roll(tot, c - w, 0)
        p = p + jnp.where(odd, up, 0.0)
        tot = tot + jnp.where(odd, up, dn)
    eq, ek = (p, tot - p) if d == 0 else (tot - p, p)
    st = st_ref[...]
    o = _dot(scores.astype(BF16), vb)
    o = o + _dot_nt(qb * jnp.exp2(eq).astype(BF16), st.astype(BF16))
    st_ref[...] = st * jnp.exp2(tot[0:1, :]) + _dot_tn(vb, kb * jnp.exp2(ek).astype(BF16))
    return o


def _hg_rec_kernel(qf_ref, zf_ref, vf_ref, qb_ref, zb_ref, vb_ref, la_ref, l1m_ref, onem_ref, mask_ref,
                   of_ref, ob_ref, stf_ref, stb_ref, *, nsub):
    @pl.when(pl.program_id(2) == 0)
    def _():
        stf_ref[...] = jnp.zeros_like(stf_ref)
        stb_ref[...] = jnp.zeros_like(stb_ref)

    la, l1m, onem = la_ref[...], l1m_ref[...], onem_ref[...]
    c = HG_CHUNK

    def body(j, carry):
        lo = pl.multiple_of(j * c, c)
        hi = pl.multiple_of((nsub - 1 - j) * c, c)
        rf = pl.ds(lo, c)
        rb = pl.ds(hi, c)
        of_ref[rf, :] = _hg_chunk(qf_ref[rf, :], zf_ref[rf, :], vf_ref[rf, :], la, l1m, onem, mask_ref, 0, stf_ref)
        ob_ref[rb, :] = _hg_chunk(qb_ref[rb, :], zb_ref[rb, :], vb_ref[rb, :], la, l1m, onem, mask_ref, 1, stb_ref)
        return carry

    lax.fori_loop(0, nsub, body, 0, unroll=True)


def _hg_post_kernel(of_ref, ob_ref, g_ref, h_ref, ng_ref, wo_ref, lg_ref, lb_ref, o_ref):
    o = of_ref[...] + ob_ref[...]
    hd = HG_HEAD_DIM
    parts = []
    for i in range(o.shape[1] // hd):
        oh = o[:, i * hd:(i + 1) * hd]
        parts.append(oh * lax.rsqrt(jnp.mean(oh * oh, axis=-1, keepdims=True) + RMS_EPS))
    on = jnp.concatenate(parts, axis=1) * ng_ref[...]
    gate = g_ref[...]
    y = on * (gate * jax.nn.sigmoid(gate))
    m = _dot(y.astype(BF16), wo_ref[...])
    o_ref[...] = _layer_norm(DEEPNORM_ALPHA * h_ref[...] + m, lg_ref[...], lb_ref[...])


def _hg_mixer_ln(h, bsz, seq, w_in, lb, norm_g, w_out, ln_g, ln_b, tm=512):
    t, d = h.shape
    hd = HG_HEAD_DIM
    nh = d // hd
    blk = min(HG_BLOCK, seq)
    nblk = seq // blk
    proj = _matmul(h, w_in.astype(BF16), tm)
    lb = lb.astype(F32).reshape(1, d)
    la, l1m, onem = jnp.log(lb), jnp.log1p(-lb), 1.0 - lb
    masks = _hg_masks(HG_CHUNK)

    def fw(col):
        return pl.BlockSpec((blk, hd), lambda b, hh, i: (b * nblk + i, col * nh + hh))

    def bw(col):
        return pl.BlockSpec((blk, hd), lambda b, hh, i: (b * nblk + nblk - 1 - i, col * nh + hh))

    vec = pl.BlockSpec((1, hd), lambda b, hh, i: (0, hh))
    o_fw, o_bw = pl.pallas_call(
        functools.partial(_hg_rec_kernel, nsub=blk // HG_CHUNK),
        grid=(bsz, nh, nblk),
        in_specs=[fw(0), fw(1), fw(3), bw(0), bw(2), bw(3), vec, vec, vec, _const_spec(masks.shape)],
        out_specs=[pl.BlockSpec((blk, hd), lambda b, hh, i: (b * nblk + i, hh)),
                   pl.BlockSpec((blk, hd), lambda b, hh, i: (b * nblk + nblk - 1 - i, hh))],
        out_shape=[jax.ShapeDtypeStruct((t, d), F32)] * 2,
        scratch_shapes=[pltpu.VMEM((hd, hd), F32), pltpu.VMEM((hd, hd), F32)],
        compiler_params=_params("parallel", "parallel", "arbitrary"),
        name="hg_rec",
    )(proj, proj, proj, proj, proj, proj, la, l1m, onem, masks)

    row = lambda v: v.reshape(1, d)
    tile = pl.BlockSpec((tm, d), lambda i: (i, 0))
    return pl.pallas_call(
        _hg_post_kernel,
        grid=(t // tm,),
        in_specs=[tile, tile, pl.BlockSpec((tm, d), lambda i: (i, 4)), tile, _const_spec((1, d)),
                  _const_spec((d, d)), _const_spec((1, d)), _const_spec((1, d))],
        out_specs=tile,
        out_shape=jax.ShapeDtypeStruct((t, d), F32),
        compiler_params=_params("parallel"),
        name="hg_post",
    )(o_fw, o_bw, proj, h, jnp.tile(norm_g.astype(F32), nh).reshape(1, d), w_out.astype(BF16), row(ln_g), row(ln_b))


def _mla_proj_kernel(x_ref, pos_ref, winT_ref, gq_ref, gkv_ref, wqT_ref, wkvT_ref, invf_ref,
                     qT_ref, k_ref, vT_ref):
    ql, kvl, half = MLA_Q_LORA, MLA_KV_LORA, MLA_ROPE // 2
    dq = MLA_NOPE + MLA_ROPE
    lat = _dot_nt(winT_ref[...], x_ref[...].astype(BF16))
    q_lat, kv_lat, k_rope = lat[:ql], lat[ql:ql + kvl], lat[ql + kvl:]

    def rms(v, g):
        return (v * lax.rsqrt(jnp.mean(v * v, axis=0, keepdims=True) + RMS_EPS) * g).astype(BF16)

    q = _dot(wqT_ref[...], rms(q_lat, gq_ref[...]))
    kv = _dot(wkvT_ref[...], rms(kv_lat, gkv_ref[...]))
    ang = invf_ref[...] * pos_ref[0].astype(F32)
    cos, sin = jnp.cos(ang), jnp.sin(ang)

    def rope(t1, t2):
        return t1 * cos - t2 * sin, t1 * sin + t2 * cos

    scale = dq ** -0.5 * math.log2(math.e)
    ones_rows = (lax.broadcasted_iota(jnp.int32, (16, lat.shape[1]), 0) == 0).astype(BF16)
    k1, k2 = rope(k_rope[:half], k_rope[half:])
    pad = jnp.zeros((MLA_QK_PAD - dq, lat.shape[1]), F32)
    k_tail = jnp.concatenate([k1, k2, pad], axis=0).T.astype(BF16)
    for hh in range(MLA_HEADS):
        qh = q[hh * dq:(hh + 1) * dq]
        q1, q2 = rope(qh[MLA_NOPE:MLA_NOPE + half], qh[MLA_NOPE + half:])
        qT_ref[0, hh, 0:MLA_NOPE, :] = (qh[:MLA_NOPE] * scale).astype(BF16)
        qT_ref[0, hh, MLA_NOPE:MLA_NOPE + half, :] = (q1 * scale).astype(BF16)
        qT_ref[0, hh, MLA_NOPE + half:dq, :] = (q2 * scale).astype(BF16)
        qT_ref[0, hh, dq:, :] = pad.astype(BF16)
        kvh = kv[hh * (MLA_NOPE + MLA_V):(hh + 1) * (MLA_NOPE + MLA_V)]
        k_ref[0, hh, 0, :, 0:MLA_NOPE] = kvh[:MLA_NOPE].T.astype(BF16)
        k_ref[0, hh, 0, :, MLA_NOPE:] = k_tail
        vT_ref[0, hh, 0, 0:MLA_V, :] = kvh[MLA_NOPE:].astype(BF16)
        vT_ref[0, hh, 0, MLA_V:, :] = ones_rows


def _attn_kernel(qT_ref, k_ref, vT_ref, o_ref, s0_sc, s1_sc, t0_sc, t1_sc, m_sc, acc_sc, *, nkv, tqs, tks):
    dv = MLA_V
    s_sc, t_sc = (s0_sc, s1_sc), (t0_sc, t1_sc)
    tk, tq = s0_sc.shape
    m_sc[...] = jnp.full_like(m_sc, -jnp.inf)
    acc_sc[...] = jnp.zeros_like(acc_sc)

    def scores(j, par):
        for c in range(tq // tqs):
            cols = slice(c * tqs, (c + 1) * tqs)
            s = _dot(k_ref[0, 0, j], qT_ref[0, 0, :, cols])
            s_sc[par][:, cols] = s
            t_sc[par][:, cols] = jnp.max(s, axis=0, keepdims=True)

    def softmax_pv(j, par):
        for c in range(tq // tqs):
            cols = slice(c * tqs, (c + 1) * tqs)
            m_prev = m_sc[:, cols]
            m_new = jnp.maximum(m_prev, t_sc[par][:, cols])
            acc = jnp.exp2(m_prev - m_new) * acc_sc[:, cols]
            for r in range(tk // tks):
                rows = slice(r * tks, (r + 1) * tks)
                p = jnp.exp2(s_sc[par][rows, cols] - m_new).astype(BF16)
                acc = acc + _dot(vT_ref[0, 0, j, :, rows], p)
            acc_sc[:, cols] = acc
            m_sc[:, cols] = m_new

    scores(0, 0)

    def body(jj, carry):
        j = 2 * jj
        scores(j + 1, 1)
        softmax_pv(j, 0)
        scores(j + 2, 0)
        softmax_pv(j + 1, 1)
        return carry

    lax.fori_loop(0, nkv // 2 - 1, body, 0, unroll=3)
    scores(nkv - 1, 1)
    softmax_pv(nkv - 2, 0)
    softmax_pv(nkv - 1, 1)
    acc = acc_sc[...]
    o_ref[0, 0] = (acc[:dv] / acc[dv:dv + 1]).astype(o_ref.dtype)


def _mla_out_kernel(oT_ref, h_ref, wo_ref, g_ref, b_ref, o_ref):
    oT = oT_ref[0]
    oT = oT.reshape(oT.shape[0] * oT.shape[1], oT.shape[2])
    m = _dot_tn(oT, wo_ref[...])
    o_ref[...] = _layer_norm(DEEPNORM_ALPHA * h_ref[...] + m, g_ref[...], b_ref[...])


def _mla_mixer_ln(h, positions, bsz, seq, w_in, q_norm_g, w_q_b, kv_norm_g, w_kv_b, w_out, ln_g, ln_b,
                  tm=512, tq=4096, tqs=256, tks=256):
    t, d = h.shape
    nh, dq, dv = MLA_HEADS, MLA_NOPE + MLA_ROPE, MLA_V
    half = MLA_ROPE // 2
    inv_freq = 1.0 / (ROPE_THETA ** (jnp.arange(half, dtype=F32) * (2.0 / MLA_ROPE)))
    nlat = w_in.shape[1]
    nt = seq // tm
    dp = MLA_QK_PAD
    qT, kc, vT = pl.pallas_call(
        _mla_proj_kernel,
        grid=(bsz, nt),
        in_specs=[pl.BlockSpec((tm, d), lambda b, i: (b * nt + i, 0)),
                  pl.BlockSpec((1, 1, tm), lambda b, i: (b, 0, i)),
                  _const_spec((nlat, d)), _const_spec((MLA_Q_LORA, 1)), _const_spec((MLA_KV_LORA, 1)),
                  _const_spec((nh * dq, MLA_Q_LORA)), _const_spec((nh * (MLA_NOPE + dv), MLA_KV_LORA)),
                  _const_spec((half, 1))],
        out_specs=[pl.BlockSpec((1, nh, dp, tm), lambda b, i: (b, 0, 0, i)),
                   pl.BlockSpec((1, nh, 1, tm, dp), lambda b, i: (b, 0, i, 0, 0)),
                   pl.BlockSpec((1, nh, 1, dv + 16, tm), lambda b, i: (b, 0, i, 0, 0))],
        out_shape=[jax.ShapeDtypeStruct((bsz, nh, dp, seq), BF16),
                   jax.ShapeDtypeStruct((bsz, nh, nt, tm, dp), BF16),
                   jax.ShapeDtypeStruct((bsz, nh, nt, dv + 16, tm), BF16)],
        compiler_params=_params("parallel", "parallel"),
        name="mla_proj",
    )(h, positions.reshape(bsz, 1, seq), w_in.T.astype(BF16), q_norm_g.astype(F32).reshape(-1, 1),
      kv_norm_g.astype(F32).reshape(-1, 1), w_q_b.T.astype(BF16), w_kv_b.T.astype(BF16),
      inv_freq.reshape(half, 1))

    tk, nkv = tm, nt
    tq = min(tq, seq)
    oT = pl.pallas_call(
        functools.partial(_attn_kernel, nkv=nkv, tqs=tqs, tks=tks),
        grid=(bsz, nh, seq // tq),
        in_specs=[pl.BlockSpec((1, 1, dp, tq), lambda b, hh, i: (b, hh, 0, i)),
                  pl.BlockSpec((1, 1, nkv, tk, dp), lambda b, hh, i: (b, hh, 0, 0, 0),
                               pipeline_mode=pl.Buffered(1)),
                  pl.BlockSpec((1, 1, nkv, dv + 16, tk), lambda b, hh, i: (b, hh, 0, 0, 0),
                               pipeline_mode=pl.Buffered(1))],
        out_specs=pl.BlockSpec((1, 1, dv, tq), lambda b, hh, i: (b, hh, 0, i)),
        out_shape=jax.ShapeDtypeStruct((bsz, nh, dv, seq), BF16),
        scratch_shapes=[pltpu.VMEM((tk, tq), F32), pltpu.VMEM((tk, tq), F32),
                        pltpu.VMEM((1, tq), F32), pltpu.VMEM((1, tq), F32),
                        pltpu.VMEM((1, tq), F32), pltpu.VMEM((dv + 16, tq), F32)],
        compiler_params=_params("parallel", "parallel", "arbitrary"),
        name="mla_attn",
    )(qT, kc, vT)

    return pl.pallas_call(
        _mla_out_kernel,
        grid=(bsz, nt),
        in_specs=[pl.BlockSpec((1, nh, dv, tm), lambda b, i: (b, 0, 0, i)),
                  pl.BlockSpec((tm, d), lambda b, i: (b * nt + i, 0)),
                  _const_spec((nh * dv, d)), _const_spec((1, d)), _const_spec((1, d))],
        out_specs=pl.BlockSpec((tm, d), lambda b, i: (b * nt + i, 0)),
        out_shape=jax.ShapeDtypeStruct((t, d), F32),
        compiler_params=_params("parallel", "parallel"),
        name="mla_out",
    )(oT, h, w_out.astype(BF16), ln_g.reshape(1, d), ln_b.reshape(1, d))


def kernel(x, positions, ln_mix_g, ln_mix_b, ln_ffn_g, ln_ffn_b, ffn_w_in, ffn_w_out, s5_w_in, s5_lam_re, s5_lam_im, s5_log_step, s5_b_re, s5_b_im, s5_c_re, s5_c_im, s5_d, s5_w_glu, s5_b_glu, s5_w_out, hg_w_in, hg_lower_bound, hg_norm_g, hg_w_out, mla_w_in, mla_q_norm_g, mla_w_q_b, mla_kv_norm_g, mla_w_kv_b, mla_w_out):
    bsz, seq, d = x.shape
    lbs = jax.nn.softmax(hg_lower_bound.astype(F32), axis=0)
    lbs = jnp.cumsum(lbs, axis=0) - lbs[0]
    h = x.reshape(bsz * seq, d)
    for layer in range(DEPTH):
        kind = layer % N_MIXERS
        slot = layer // N_MIXERS
        lg, lb = ln_mix_g[layer], ln_mix_b[layer]
        if kind == 0:
            h = _s5_mixer_ln(h, bsz, seq, s5_w_in[slot], s5_lam_re[slot], s5_lam_im[slot], s5_log_step[slot],
                             s5_b_re[slot], s5_b_im[slot], s5_c_re[slot], s5_c_im[slot], s5_d[slot],
                             s5_w_glu[slot], s5_b_glu[slot], s5_w_out[slot], lg, lb)
        elif kind == 1:
            h = _hg_mixer_ln(h, bsz, seq, hg_w_in[slot], lbs[layer], hg_norm_g[slot], hg_w_out[slot], lg, lb)
        else:
            h = _mla_mixer_ln(h, positions, bsz, seq, mla_w_in[slot], mla_q_norm_g[slot], mla_w_q_b[slot],
                              mla_kv_norm_g[slot], mla_w_kv_b[slot], mla_w_out[slot], lg, lb)
        h = _ffn_ln(h, ffn_w_in[layer], ffn_w_out[layer], ln_ffn_g[layer], ln_ffn_b[layer])
    return h.reshape(bsz, seq, d)
```

```python
import functools
import math

import jax
import jax.numpy as jnp
from jax import lax
from jax.experimental import pallas as pl
from jax.experimental.pallas import tpu as pltpu

F32 = jnp.float32
BF16 = jnp.bfloat16

DEPTH = 4
N_MIXERS = 3
S5_GROUP = 16
S5_STATE = 64
S5_CHUNK = 64
S5_PW_ROWS = 72
HG_HEAD_DIM = 128
HG_CHUNK = 128
HG_BLOCK = 512
MLA_HEADS = 8
MLA_NOPE = 128
MLA_ROPE = 64
MLA_V = 128
MLA_QK_PAD = 256
MLA_Q_LORA = 384
MLA_KV_LORA = 256
ROPE_THETA = 10000.0
DEEPNORM_ALPHA = (2 * DEPTH) ** 0.25
LN_EPS = 1e-5
RMS_EPS = 1e-6
FFN_CHUNK = 512
VMEM_LIMIT = 56 * 1024 * 1024


def _params(*sem):
    return pltpu.CompilerParams(dimension_semantics=sem, vmem_limit_bytes=VMEM_LIMIT)


def _dot(a, b):
    return jnp.dot(a, b, preferred_element_type=F32)


def _dot_nt(a, b):
    return lax.dot_general(a, b, (((1,), (1,)), ((), ())), preferred_element_type=F32)


def _dot_tn(a, b):
    return lax.dot_general(a, b, (((0,), (0,)), ((), ())), preferred_element_type=F32)


def _layer_norm(y, g, b):
    mu = jnp.mean(y, axis=-1, keepdims=True)
    d = y - mu
    var = jnp.mean(d * d, axis=-1, keepdims=True)
    return d * lax.rsqrt(var + LN_EPS) * g + b


def _const_spec(shape):
    return pl.BlockSpec(shape, lambda *_: (0,) * len(shape))


def _matmul_kernel(x_ref, w_ref, o_ref):
    o_ref[...] = _dot(x_ref[...].astype(BF16), w_ref[...]).astype(o_ref.dtype)


def _matmul(x, w, tm, out_dtype=F32):
    t, k = x.shape
    n = w.shape[1]
    return pl.pallas_call(
        _matmul_kernel,
        grid=(t // tm,),
        in_specs=[pl.BlockSpec((tm, k), lambda i: (i, 0)), _const_spec((k, n))],
        out_specs=pl.BlockSpec((tm, n), lambda i: (i, 0)),
        out_shape=jax.ShapeDtypeStruct((t, n), out_dtype),
        compiler_params=_params("parallel"),
        name="proj",
    )(x, w)


def _ffn_kernel(x_ref, win_ref, wout_ref, g_ref, b_ref, o_ref, *, hid):
    x = x_ref[...]
    xb = x.astype(BF16)
    acc = DEEPNORM_ALPHA * x
    for lo in range(0, hid, FFN_CHUNK):
        w = min(FFN_CHUNK, hid - lo)
        gate = _dot(xb, win_ref[:, lo:lo + w])
        up = _dot(xb, win_ref[:, hid + lo:hid + lo + w])
        mid = (gate * jax.nn.sigmoid(gate) * up).astype(BF16)
        acc = acc + _dot(mid, wout_ref[lo:lo + w, :])
    o_ref[...] = _layer_norm(acc, g_ref[...], b_ref[...])


def _ffn_ln(x, w_in, w_out, g, b, tm=512):
    t, d = x.shape
    hid = w_out.shape[0]
    return pl.pallas_call(
        functools.partial(_ffn_kernel, hid=hid),
        grid=(t // tm,),
        in_specs=[pl.BlockSpec((tm, d), lambda i: (i, 0)),
                  pl.BlockSpec((d, 2 * hid), lambda i: (0, 0), pipeline_mode=pl.Buffered(1)),
                  pl.BlockSpec((hid, d), lambda i: (0, 0), pipeline_mode=pl.Buffered(1)),
                  _const_spec((1, d)), _const_spec((1, d))],
        out_specs=pl.BlockSpec((tm, d), lambda i: (i, 0)),
        out_shape=jax.ShapeDtypeStruct((t, d), F32),
        compiler_params=_params("parallel"),
        name="ffn_ln",
    )(x, w_in.astype(BF16), w_out.astype(BF16), g.reshape(1, d), b.reshape(1, d))


def _s5_tables(lam_re, lam_im, log_step, b_re, b_im, c_re, c_im, nlev):
    L = S5_CHUNK
    lam = lax.complex(lam_re.astype(F32), lam_im.astype(F32))
    step = jnp.exp(log_step.astype(F32))[..., None]
    lam_dt = lam * step
    lam_bar = jnp.exp(lam_dt)
    b_bar = ((lam_bar - 1.0) / lam)[..., None] * lax.complex(b_re.astype(F32), b_im.astype(F32))
    g, p = lam.shape[1], lam.shape[2]
    taus = jnp.arange(S5_PW_ROWS, dtype=F32)
    pw = jnp.exp(lam_dt[:, None] * taus[None, :, None, None])
    pr, pi = jnp.real(pw), jnp.imag(pw)
    cat = lambda a, b: jnp.concatenate([a, b], axis=-1)
    pw_tab = jnp.stack([cat(pr, pr), cat(pi, pi)], axis=1)
    pw_tab = pw_tab.transpose(3, 0, 1, 2, 4).reshape(g, 4, S5_PW_ROWS, 2 * p)
    bt = jnp.swapaxes(b_bar, 2, 3)
    bre, bim = jnp.real(bt), jnp.imag(bt)
    cre, cim = c_re.astype(F32), c_im.astype(F32)
    c2 = cat(cre, -cim)
    bc_tab = jnp.stack([cat(bre, bim), cat(-bim, bre), c2, cat(-cim, -cre)], axis=2)
    bc_tab = bc_tab.transpose(1, 0, 2, 3, 4).reshape(g, 8, S5_GROUP, 2 * p)
    cmat = cat(c2[0], c2[1]).astype(BF16)
    k2 = (L * 2.0 ** jnp.arange(nlev, dtype=F32))
    a = jnp.exp(lam_dt[:, None] * k2[None, :, None, None])
    ar, ai = jnp.real(a), jnp.imag(a)
    p1 = cat(ar, ar).transpose(2, 0, 3, 1).reshape(g, 4 * p, nlev)
    p2 = cat(-ai, ai).transpose(2, 0, 3, 1).reshape(g, 4 * p, nlev)
    return pw_tab, bc_tab, cmat, p1, p2


def _s5_core_kernel(u_ref, pw_ref, bc_ref, cm_ref, p1_ref, p2_ref, y_ref, ut_sc, e_sc, gt_sc, bs_sc, tt_sc,
                    *, nc, nlev):
    L, hh, p, two_p = S5_CHUNK, S5_GROUP, S5_STATE, 2 * S5_STATE
    lh = L * hh
    n = ut_sc.shape[1]

    def blk(d, kind, tau):
        base = 4 * d + 2 * kind
        return (bc_ref[0, base] * pw_ref[0, 2 * d, tau:tau + 1, :]
                + bc_ref[0, base + 1] * pw_ref[0, 2 * d + 1, tau:tau + 1, :])

    zero = jnp.zeros((hh, two_p), BF16)
    for i in range(L):
        rows = slice(i * hh, (i + 1) * hh)
        ut_sc[rows, :] = u_ref[:, i * n:(i + 1) * n]
        e_sc[rows, :two_p] = blk(0, 0, L - 1 - i).astype(BF16)
        e_sc[rows, two_p:] = blk(1, 0, i).astype(BF16)
        gt_sc[rows, :two_p] = blk(0, 1, i + 1).astype(BF16)
        gt_sc[rows, two_p:] = blk(1, 1, L - i).astype(BF16)
    for rho in range(2 * L):
        rows = slice(rho * hh, (rho + 1) * hh)
        lag = L - 1 - rho
        bs_sc[rows, :two_p] = blk(0, 0, lag).astype(BF16) if 0 <= lag < L else zero
        bs_sc[rows, two_p:] = blk(1, 0, -lag).astype(BF16) if 0 <= -lag < L else zero
    strip = _dot_nt(cm_ref[0], bs_sc[...])
    for r in range(128 // hh):
        rot = strip if r == 0 else pltpu.roll(strip, 2 * lh - r * hh, 1)
        for j in range(L):
            a, rr = divmod(L - 1 - j, 128 // hh)
            if rr == r:
                tt_sc[j * hh:(j + 1) * hh, :] = rot[:, 128 * a:128 * a + lh].astype(BF16)

    ut = ut_sc[...]
    y = _dot(tt_sc[...], ut)
    s = _dot_tn(e_sc[...], ut)
    cidx = lax.broadcasted_iota(jnp.int32, (two_p, n), 1) & (nc - 1)

    def cmul(x, k, d):
        a1 = p1_ref[0, d * two_p:(d + 1) * two_p, k:k + 1]
        a2 = p2_ref[0, d * two_p:(d + 1) * two_p, k:k + 1]
        return x * a1 + jnp.concatenate([x[p:], x[:p]], axis=0) * a2

    xf, xb = s[:two_p], s[two_p:]
    for k in range(nlev):
        sh = 1 << k
        xf = xf + jnp.where(cidx >= sh, cmul(pltpu.roll(xf, sh, 1), k, 0), 0.0)
        xb = xb + jnp.where(cidx < nc - sh, cmul(pltpu.roll(xb, n - sh, 1), k, 1), 0.0)
    hf = jnp.where(cidx >= 1, pltpu.roll(xf, 1, 1), 0.0)
    hb = jnp.where(cidx < nc - 1, pltpu.roll(xb, n - 1, 1), 0.0)
    y = y + _dot(gt_sc[...], jnp.concatenate([hf, hb], axis=0).astype(BF16))
    for j in range(L):
        y_ref[:, j * n:(j + 1) * n] = y[j * hh:(j + 1) * hh, :]


def _s5_core(ut, tables, n, nc, nlev):
    d, t = ut.shape
    hh, two_p = S5_GROUP, 2 * S5_STATE
    lh = S5_CHUNK * hh
    blk = lambda a: pl.BlockSpec((1,) + a.shape[1:], lambda i: (i,) + (0,) * (a.ndim - 1))
    return pl.pallas_call(
        functools.partial(_s5_core_kernel, nc=nc, nlev=nlev),
        grid=(d // hh,),
        in_specs=[pl.BlockSpec((hh, t), lambda i: (i, 0))] + [blk(a) for a in tables],
        out_specs=pl.BlockSpec((hh, t), lambda i: (i, 0)),
        out_shape=jax.ShapeDtypeStruct((d, t), F32),
        scratch_shapes=[pltpu.VMEM((lh, n), BF16),
                        pltpu.VMEM((lh, 2 * two_p), BF16), pltpu.VMEM((lh, 2 * two_p), BF16),
                        pltpu.VMEM((2 * lh, 2 * two_p), BF16), pltpu.VMEM((lh, lh), BF16)],
        compiler_params=_params("parallel"),
        name="s5_core",
    )(ut, *tables)


def _s5_in_kernel(x_ref, wT_ref, o_ref):
    o_ref[...] = _dot_nt(wT_ref[...], x_ref[...].astype(BF16)).astype(o_ref.dtype)


def _s5_post_kernel(yT_ref, x_ref, winT_ref, d_ref, wgT_ref, bg_ref, wo_ref, g_ref, b_ref, o_ref):
    x = x_ref[...]
    u = _dot_nt(winT_ref[...], x.astype(BF16))
    y = yT_ref[...] + d_ref[...] * u
    y = jax.nn.gelu(y)
    z = _dot(wgT_ref[...], y.astype(BF16)) + bg_ref[...]
    y = y * jax.nn.sigmoid(z)
    m = _dot_tn(y.astype(BF16), wo_ref[...])
    o_ref[...] = _layer_norm(DEEPNORM_ALPHA * x + m, g_ref[...], b_ref[...])


def _s5_mixer_ln(h, bsz, seq, w_in, lam_re, lam_im, log_step, b_re, b_im, c_re, c_im,
                 d_skip, w_glu, b_glu, w_out, ln_g, ln_b, tm=512):
    t, d = h.shape
    L = S5_CHUNK
    nc = seq // L
    n = bsz * nc
    nlev = max(1, (nc - 1).bit_length())
    assert nc & (nc - 1) == 0
    hp = h.reshape(n, L, d).transpose(1, 0, 2).reshape(t, d)
    w_in_t = w_in.T.astype(BF16)
    tile = pl.BlockSpec((tm, d), lambda i: (i, 0))
    tile_t = pl.BlockSpec((d, tm), lambda i: (0, i))
    ut = pl.pallas_call(
        _s5_in_kernel,
        grid=(t // tm,),
        in_specs=[tile, _const_spec((d, d))],
        out_specs=tile_t,
        out_shape=jax.ShapeDtypeStruct((d, t), BF16),
        compiler_params=_params("parallel"),
        name="s5_in",
    )(hp, w_in_t)
    tables = _s5_tables(lam_re, lam_im, log_step, b_re, b_im, c_re, c_im, nlev)
    yt = _s5_core(ut, tables, n, nc, nlev)
    row = lambda v: v.reshape(1, d)
    col = lambda v: v.astype(F32).reshape(d, 1)
    out = pl.pallas_call(
        _s5_post_kernel,
        grid=(t // tm,),
        in_specs=[tile_t, tile, _const_spec((d, d)), _const_spec((d, 1)), _const_spec((d, d)), _const_spec((d, 1)),
                  _const_spec((d, d)), _const_spec((1, d)), _const_spec((1, d))],
        out_specs=tile,
        out_shape=jax.ShapeDtypeStruct((t, d), F32),
        compiler_params=_params("parallel"),
        name="s5_post",
    )(yt, hp, w_in_t, col(d_skip), w_glu.T.astype(BF16), col(b_glu), w_out.astype(BF16), row(ln_g), row(ln_b))
    return out.reshape(L, n, d).transpose(1, 0, 2).reshape(t, d)


def _hg_masks(c):
    nlev = c.bit_length() - 1
    t = jnp.arange(c)[:, None]
    s = jnp.arange(c)[None, :]
    ms = []
    for lev in range(nlev):
        w = 1 << lev
        ms.append(((t & w) != 0) & ((s & w) == 0) & ((t >> (lev + 1)) == (s >> (lev + 1))))
    ms.append(t == s)
    fw = jnp.stack(ms).astype(F32)
    return jnp.stack([fw, fw.transpose(0, 2, 1)])


def _hg_gate(z, la, l1m, onem):
    ls = jnp.minimum(z, 0.0) - jnp.log1p(jnp.exp(-jnp.abs(z)))
    b = l1m + ls
    lf = jnp.maximum(la, b) + jnp.log1p(jnp.exp(-jnp.abs(la - b)))
    return lf, onem * jax.nn.sigmoid(-z)


def _hg_chunk(q, z, v, la, l1m, onem, mask_ref, d, st_ref):
    c = q.shape[0]
    nlev = c.bit_length() - 1
    q = q * jax.nn.sigmoid(q)
    lf, k = _hg_gate(z, la, l1m, onem)
    lf = lf * math.log2(math.e)
    qb, kb, vb = q.astype(BF16), k.astype(BF16), v.astype(BF16)
    row = lax.broadcasted_iota(jnp.int32, lf.shape, 0)
    scores = mask_ref[d, nlev] * _dot_nt(qb, kb)
    p, tot = (lf if d == 0 else jnp.zeros_like(lf)), lf
    for lev in range(min(nlev, 3)):
        w = 1 << lev
        odd = (row & w) != 0
        f = jnp.exp2(jnp.where(odd, p, tot - p)).astype(BF16)
        scores = scores + mask_ref[d, lev] * _dot_nt(qb * f, kb * f)
        up = pltpu.roll(tot, w, 0)
        dn = pltpu.roll(tot, c - w, 0)
        p = p + jnp.where(odd, up, 0.0)
        tot = tot + jnp.where(odd, up, dn)
    for lev in range(3, nlev):
        w = 1 << lev
        nb = c // w
        cut = lambda x: [x[i * w:(i + 1) * w] for i in range(nb)]
        cat = lambda xs: jnp.concatenate(xs, axis=0)
        pb, tb, sb = cut(p), cut(tot), cut(scores)
        f = jnp.exp2(cat([pb[i] if i % 2 else tb[i] - pb[i] for i in range(nb)])).astype(BF16)
        q_side = lambda i: (i % 2 == 1) == (d == 0)
        if w >= 16:
            x = cat([a if q_side(i) else b for i, (a, b) in enumerate(zip(cut(qb), cut(kb)))])
        else:
            x = cat([a if q_side(i) else b for i, (a, b) in enumerate(zip(cut(q), cut(k)))]).astype(BF16)
        x = x * f
        db = cut(_dot_nt(x, x))
        scores = cat([sb[i] + mask_ref[d, lev, i * w:(i + 1) * w, :] * db[i] if q_side(i) else sb[i]
                      for i in range(nb)])
        p = cat([pb[i] + tb[i - 1] if i % 2 else pb[i] for i in range(nb)])
        tot = cat([tb[i] + tb[i ^ 1] for i in range(nb)])
    eq, ek = (p, tot - p) if d == 0 else (tot - p, p)
    st = st_ref[...]
    o = _dot(scores.astype(BF16), vb)
    o = o + _dot_nt(qb * jnp.exp2(eq).astype(BF16), st.astype(BF16))
    st_ref[...] = st * jnp.exp2(tot[0:1, :]) + _dot_tn(vb, kb * jnp.exp2(ek).astype(BF16))
    return o


def _hg_rec_kernel(qf_ref, zf_ref, vf_ref, qb_ref, zb_ref, vb_ref, la_ref, l1m_ref, onem_ref, mask_ref,
                   of_ref, ob_ref, stf_ref, stb_ref, *, nsub):
    @pl.when(pl.program_id(2) == 0)
    def _():
        stf_ref[...] = jnp.zeros_like(stf_ref)
        stb_ref[...] = jnp.zeros_like(stb_ref)

    la, l1m, onem = la_ref[...], l1m_ref[...], onem_ref[...]
    c = HG_CHUNK

    def body(j, carry):
        lo = pl.multiple_of(j * c, c)
        hi = pl.multiple_of((nsub - 1 - j) * c, c)
        rf = pl.ds(lo, c)
        rb = pl.ds(hi, c)
        of_ref[rf, :] = _hg_chunk(qf_ref[rf, :], zf_ref[rf, :], vf_ref[rf, :], la, l1m, onem, mask_ref, 0, stf_ref)
        ob_ref[rb, :] = _hg_chunk(qb_ref[rb, :], zb_ref[rb, :], vb_ref[rb, :], la, l1m, onem, mask_ref, 1, stb_ref)
        return carry

    lax.fori_loop(0, nsub, body, 0, unroll=True)


def _hg_post_kernel(of_ref, ob_ref, g_ref, h_ref, ng_ref, wo_ref, lg_ref, lb_ref, o_ref):
    o = of_ref[...] + ob_ref[...]
    hd = HG_HEAD_DIM
    parts = []
    for i in range(o.shape[1] // hd):
        oh = o[:, i * hd:(i + 1) * hd]
        parts.append(oh * lax.rsqrt(jnp.mean(oh * oh, axis=-1, keepdims=True) + RMS_EPS))
    on = jnp.concatenate(parts, axis=1) * ng_ref[...]
    gate = g_ref[...]
    y = on * (gate * jax.nn.sigmoid(gate))
    m = _dot(y.astype(BF16), wo_ref[...])
    o_ref[...] = _layer_norm(DEEPNORM_ALPHA * h_ref[...] + m, lg_ref[...], lb_ref[...])


def _hg_mixer_ln(h, bsz, seq, w_in, lb, norm_g, w_out, ln_g, ln_b, tm=512):
    t, d = h.shape
    hd = HG_HEAD_DIM
    nh = d // hd
    blk = min(HG_BLOCK, seq)
    nblk = seq // blk
    proj = _matmul(h, w_in.astype(BF16), tm)
    lb = lb.astype(F32).reshape(1, d)
    la, l1m, onem = jnp.log(lb), jnp.log1p(-lb), 1.0 - lb
    masks = _hg_masks(HG_CHUNK)

    def fw(col):
        return pl.BlockSpec((blk, hd), lambda b, hh, i: (b * nblk + i, col * nh + hh))

    def bw(col):
        return pl.BlockSpec((blk, hd), lambda b, hh, i: (b * nblk + nblk - 1 - i, col * nh + hh))

    vec = pl.BlockSpec((1, hd), lambda b, hh, i: (0, hh))
    o_fw, o_bw = pl.pallas_call(
        functools.partial(_hg_rec_kernel, nsub=blk // HG_CHUNK),
        grid=(bsz, nh, nblk),
        in_specs=[fw(0), fw(1), fw(3), bw(0), bw(2), bw(3), vec, vec, vec, _const_spec(masks.shape)],
        out_specs=[pl.BlockSpec((blk, hd), lambda b, hh, i: (b * nblk + i, hh)),
                   pl.BlockSpec((blk, hd), lambda b, hh, i: (b * nblk + nblk - 1 - i, hh))],
        out_shape=[jax.ShapeDtypeStruct((t, d), F32)] * 2,
        scratch_shapes=[pltpu.VMEM((hd, hd), F32), pltpu.VMEM((hd, hd), F32)],
        compiler_params=_params("parallel", "parallel", "arbitrary"),
        name="hg_rec",
    )(proj, proj, proj, proj, proj, proj, la, l1m, onem, masks)

    row = lambda v: v.reshape(1, d)
    tile = pl.BlockSpec((tm, d), lambda i: (i, 0))
    return pl.pallas_call(
        _hg_post_kernel,
        grid=(t // tm,),
        in_specs=[tile, tile, pl.BlockSpec((tm, d), lambda i: (i, 4)), tile, _const_spec((1, d)),
                  _const_spec((d, d)), _const_spec((1, d)), _const_spec((1, d))],
        out_specs=tile,
        out_shape=jax.ShapeDtypeStruct((t, d), F32),
        compiler_params=_params("parallel"),
        name="hg_post",
    )(o_fw, o_bw, proj, h, jnp.tile(norm_g.astype(F32), nh).reshape(1, d), w_out.astype(BF16), row(ln_g), row(ln_b))


def _mla_proj_kernel(x_ref, pos_ref, winT_ref, gq_ref, gkv_ref, wqT_ref, wkvT_ref, invf_ref,
                     qT_ref, k_ref, vT_ref):
    ql, kvl, half = MLA_Q_LORA, MLA_KV_LORA, MLA_ROPE // 2
    dq = MLA_NOPE + MLA_ROPE
    lat = _dot_nt(winT_ref[...], x_ref[...].astype(BF16))
    q_lat, kv_lat, k_rope = lat[:ql], lat[ql:ql + kvl], lat[ql + kvl:]

    def rms(v, g):
        return (v * lax.rsqrt(jnp.mean(v * v, axis=0, keepdims=True) + RMS_EPS) * g).astype(BF16)

    q = _dot(wqT_ref[...], rms(q_lat, gq_ref[...]))
    kv = _dot(wkvT_ref[...], rms(kv_lat, gkv_ref[...]))
    ang = invf_ref[...] * pos_ref[0].astype(F32)
    cos, sin = jnp.cos(ang), jnp.sin(ang)

    def rope(t1, t2):
        return t1 * cos - t2 * sin, t1 * sin + t2 * cos

    scale = dq ** -0.5 * math.log2(math.e)
    ones_rows = (lax.broadcasted_iota(jnp.int32, (16, lat.shape[1]), 0) == 0).astype(BF16)
    k1, k2 = rope(k_rope[:half], k_rope[half:])
    pad = jnp.zeros((MLA_QK_PAD - dq, lat.shape[1]), F32)
    k_tail = jnp.concatenate([k1, k2, pad], axis=0).T.astype(BF16)
    for hh in range(MLA_HEADS):
        qh = q[hh * dq:(hh + 1) * dq]
        q1, q2 = rope(qh[MLA_NOPE:MLA_NOPE + half], qh[MLA_NOPE + half:])
        qT_ref[0, hh, 0:MLA_NOPE, :] = (qh[:MLA_NOPE] * scale).astype(BF16)
        qT_ref[0, hh, MLA_NOPE:MLA_NOPE + half, :] = (q1 * scale).astype(BF16)
        qT_ref[0, hh, MLA_NOPE + half:dq, :] = (q2 * scale).astype(BF16)
        qT_ref[0, hh, dq:, :] = pad.astype(BF16)
        kvh = kv[hh * (MLA_NOPE + MLA_V):(hh + 1) * (MLA_NOPE + MLA_V)]
        k_ref[0, hh, 0, :, 0:MLA_NOPE] = kvh[:MLA_NOPE].T.astype(BF16)
        k_ref[0, hh, 0, :, MLA_NOPE:] = k_tail
        vT_ref[0, hh, 0, 0:MLA_V, :] = kvh[MLA_NOPE:].astype(BF16)
        vT_ref[0, hh, 0, MLA_V:, :] = ones_rows


def _attn_kernel(qT_ref, k_ref, vT_ref, o_ref, s0_sc, s1_sc, t0_sc, t1_sc, m_sc, acc_sc, *, nkv, tqs, tks):
    dv = MLA_V
    s_sc, t_sc = (s0_sc, s1_sc), (t0_sc, t1_sc)
    tk, tq = s0_sc.shape
    m_sc[...] = jnp.full_like(m_sc, -jnp.inf)
    acc_sc[...] = jnp.zeros_like(acc_sc)

    def scores(j, par, c):
        cols = slice(c * tqs, (c + 1) * tqs)
        s = _dot(k_ref[0, 0, j], qT_ref[0, 0, :, cols])
        s_sc[par][:, cols] = s
        t_sc[par][:, cols] = jnp.max(s, axis=0, keepdims=True)

    def softmax_pv(j, par, c):
        cols = slice(c * tqs, (c + 1) * tqs)
        m_prev = m_sc[:, cols]
        m_new = jnp.maximum(m_prev, t_sc[par][:, cols])
        acc = jnp.exp2(m_prev - m_new) * acc_sc[:, cols]
        for r in range(tk // tks):
            rows = slice(r * tks, (r + 1) * tks)
            p = jnp.exp2(s_sc[par][rows, cols] - m_new).astype(BF16)
            acc = acc + _dot(vT_ref[0, 0, j, :, rows], p)
        acc_sc[:, cols] = acc
        m_sc[:, cols] = m_new

    nsub = tq // tqs

    def step(j_scores, j_soft):
        for c in range(nsub):
            if j_scores is not None:
                scores(j_scores[0], j_scores[1], c)
            if j_soft is not None:
                softmax_pv(j_soft[0], j_soft[1], c)

    step((0, 0), None)

    def body(jj, carry):
        j = 2 * jj
        step((j + 1, 1), (j, 0))
        step((j + 2, 0), (j + 1, 1))
        return carry

    lax.fori_loop(0, nkv // 2 - 1, body, 0, unroll=3)
    step((nkv - 1, 1), (nkv - 2, 0))
    step(None, (nkv - 1, 1))
    acc = acc_sc[...]
    o_ref[0, 0] = (acc[:dv] / acc[dv:dv + 1]).astype(o_ref.dtype)


def _mla_out_kernel(oT_ref, h_ref, wo_ref, g_ref, b_ref, o_ref):
    oT = oT_ref[0]
    oT = oT.reshape(oT.shape[0] * oT.shape[1], oT.shape[2])
    m = _dot_tn(oT, wo_ref[...])
    o_ref[...] = _layer_norm(DEEPNORM_ALPHA * h_ref[...] + m, g_ref[...], b_ref[...])


def _mla_mixer_ln(h, positions, bsz, seq, w_in, q_norm_g, w_q_b, kv_norm_g, w_kv_b, w_out, ln_g, ln_b,
                  tm=512, tq=4096, tqs=256, tks=256):
    t, d = h.shape
    nh, dq, dv = MLA_HEADS, MLA_NOPE + MLA_ROPE, MLA_V
    half = MLA_ROPE // 2
    inv_freq = 1.0 / (ROPE_THETA ** (jnp.arange(half, dtype=F32) * (2.0 / MLA_ROPE)))
    nlat = w_in.shape[1]
    nt = seq // tm
    dp = MLA_QK_PAD
    qT, kc, vT = pl.pallas_call(
        _mla_proj_kernel,
        grid=(bsz, nt),
        in_specs=[pl.BlockSpec((tm, d), lambda b, i: (b * nt + i, 0)),
                  pl.BlockSpec((1, 1, tm), lambda b, i: (b, 0, i)),
                  _const_spec((nlat, d)), _const_spec((MLA_Q_LORA, 1)), _const_spec((MLA_KV_LORA, 1)),
                  _const_spec((nh * dq, MLA_Q_LORA)), _const_spec((nh * (MLA_NOPE + dv), MLA_KV_LORA)),
                  _const_spec((half, 1))],
        out_specs=[pl.BlockSpec((1, nh, dp, tm), lambda b, i: (b, 0, 0, i)),
                   pl.BlockSpec((1, nh, 1, tm, dp), lambda b, i: (b, 0, i, 0, 0)),
                   pl.BlockSpec((1, nh, 1, dv + 16, tm), lambda b, i: (b, 0, i, 0, 0))],
        out_shape=[jax.ShapeDtypeStruct((bsz, nh, dp, seq), BF16),
                   jax.ShapeDtypeStruct((bsz, nh, nt, tm, dp), BF16),
                   jax.ShapeDtypeStruct((bsz, nh, nt, dv + 16, tm), BF16)],
        compiler_params=_params("parallel", "parallel"),
        name="mla_proj",
    )(h, positions.reshape(bsz, 1, seq), w_in.T.astype(BF16), q_norm_g.astype(F32).reshape(-1, 1),
      kv_norm_g.astype(F32).reshape(-1, 1), w_q_b.T.astype(BF16), w_kv_b.T.astype(BF16),
      inv_freq.reshape(half, 1))

    tk, nkv = tm, nt
    tq = min(tq, seq)
    oT = pl.pallas_call(
        functools.partial(_attn_kernel, nkv=nkv, tqs=tqs, tks=tks),
        grid=(bsz, nh, seq // tq),
        in_specs=[pl.BlockSpec((1, 1, dp, tq), lambda b, hh, i: (b, hh, 0, i)),
                  pl.BlockSpec((1, 1, nkv, tk, dp), lambda b, hh, i: (b, hh, 0, 0, 0),
                               pipeline_mode=pl.Buffered(1)),
                  pl.BlockSpec((1, 1, nkv, dv + 16, tk), lambda b, hh, i: (b, hh, 0, 0, 0),
                               pipeline_mode=pl.Buffered(1))],
        out_specs=pl.BlockSpec((1, 1, dv, tq), lambda b, hh, i: (b, hh, 0, i)),
        out_shape=jax.ShapeDtypeStruct((bsz, nh, dv, seq), BF16),
        scratch_shapes=[pltpu.VMEM((tk, tq), F32), pltpu.VMEM((tk, tq), F32),
                        pltpu.VMEM((1, tq), F32), pltpu.VMEM((1, tq), F32),
                        pltpu.VMEM((1, tq), F32), pltpu.VMEM((dv + 16, tq), F32)],
        compiler_params=_params("parallel", "parallel", "arbitrary"),
        name="mla_attn",
    )(qT, kc, vT)

    return pl.pallas_call(
        _mla_out_kernel,
        grid=(bsz, nt),
        in_specs=[pl.BlockSpec((1, nh, dv, tm), lambda b, i: (b, 0, 0, i)),
                  pl.BlockSpec((tm, d), lambda b, i: (b * nt + i, 0)),
                  _const_spec((nh * dv, d)), _const_spec((1, d)), _const_spec((1, d))],
        out_specs=pl.BlockSpec((tm, d), lambda b, i: (b * nt + i, 0)),
        out_shape=jax.ShapeDtypeStruct((t, d), F32),
        compiler_params=_params("parallel", "parallel"),
        name="mla_out",
    )(oT, h, w_out.astype(BF16), ln_g.reshape(1, d), ln_b.reshape(1, d))


def kernel(x, positions, ln_mix_g, ln_mix_b, ln_ffn_g, ln_ffn_b, ffn_w_in, ffn_w_out, s5_w_in, s5_lam_re, s5_lam_im, s5_log_step, s5_b_re, s5_b_im, s5_c_re, s5_c_im, s5_d, s5_w_glu, s5_b_glu, s5_w_out, hg_w_in, hg_lower_bound, hg_norm_g, hg_w_out, mla_w_in, mla_q_norm_g, mla_w_q_b, mla_kv_norm_g, mla_w_kv_b, mla_w_out):
    bsz, seq, d = x.shape
    lbs = jax.nn.softmax(hg_lower_bound.astype(F32), axis=0)
    lbs = jnp.cumsum(lbs, axis=0) - lbs[0]
    h = x.reshape(bsz * seq, d)
    for layer in range(DEPTH):
        kind = layer % N_MIXERS
        slot = layer // N_MIXERS
        lg, lb = ln_mix_g[layer], ln_mix_b[layer]
        if kind == 0:
            h = _s5_mixer_ln(h, bsz, seq, s5_w_in[slot], s5_lam_re[slot], s5_lam_im[slot], s5_log_step[slot],
                             s5_b_re[slot], s5_b_im[slot], s5_c_re[slot], s5_c_im[slot], s5_d[slot],
                             s5_w_glu[slot], s5_b_glu[slot], s5_w_out[slot], lg, lb)
        elif kind == 1:
            h = _hg_mixer_ln(h, bsz, seq, hg_w_in[slot], lbs[layer], hg_norm_g[slot], hg_w_out[slot], lg, lb)
        else:
            h = _mla_mixer_ln(h, positions, bsz, seq, mla_w_in[slot], mla_q_norm_g[slot], mla_w_q_b[slot],
                              mla_kv_norm_g[slot], mla_w_kv_b[slot], mla_w_out[slot], lg, lb)
        h = _ffn_ln(h, ffn_w_in[layer], ffn_w_out[layer], ln_ffn_g[layer], ln_ffn_b[layer])
    return h.reshape(bsz, seq, d)
```

```python
import functools
import math

import jax
import jax.numpy as jnp
from jax import lax
from jax.experimental import pallas as pl
from jax.experimental.pallas import tpu as pltpu

F32 = jnp.float32
BF16 = jnp.bfloat16

DEPTH = 4
N_MIXERS = 3
S5_GROUP = 16
S5_STATE = 64
S5_CHUNK = 64
S5_PW_ROWS = 72
HG_HEAD_DIM = 128
HG_CHUNK = 128
HG_BLOCK = 512
HG_IN_PIECE = 512
HG_INTERLEAVE = 1
MLA_HEADS = 8
MLA_NOPE = 128
MLA_ROPE = 64
MLA_V = 128
MLA_QK_PAD = 256
MLA_Q_LORA = 384
MLA_KV_LORA = 256
ROPE_THETA = 10000.0
DEEPNORM_ALPHA = (2 * DEPTH) ** 0.25
LN_EPS = 1e-5
RMS_EPS = 1e-6
FFN_CHUNK = 512
VMEM_LIMIT = 56 * 1024 * 1024


def _params(*sem):
    return pltpu.CompilerParams(dimension_semantics=sem, vmem_limit_bytes=VMEM_LIMIT)


def _dot(a, b):
    return jnp.dot(a, b, preferred_element_type=F32)


def _dot_nt(a, b):
    return lax.dot_general(a, b, (((1,), (1,)), ((), ())), preferred_element_type=F32)


def _dot_tn(a, b):
    return lax.dot_general(a, b, (((0,), (0,)), ((), ())), preferred_element_type=F32)


def _layer_norm(y, g, b):
    mu = jnp.mean(y, axis=-1, keepdims=True)
    d = y - mu
    var = jnp.mean(d * d, axis=-1, keepdims=True)
    return d * lax.rsqrt(var + LN_EPS) * g + b


def _const_spec(shape):
    return pl.BlockSpec(shape, lambda *_: (0,) * len(shape))


def _matmul_kernel(x_ref, w_ref, o_ref):
    o_ref[...] = _dot(x_ref[...].astype(BF16), w_ref[...]).astype(o_ref.dtype)


def _matmul(x, w, tm, out_dtype=F32):
    t, k = x.shape
    n = w.shape[1]
    return pl.pallas_call(
        _matmul_kernel,
        grid=(t // tm,),
        in_specs=[pl.BlockSpec((tm, k), lambda i: (i, 0)), _const_spec((k, n))],
        out_specs=pl.BlockSpec((tm, n), lambda i: (i, 0)),
        out_shape=jax.ShapeDtypeStruct((t, n), out_dtype),
        compiler_params=_params("parallel"),
        name="proj",
    )(x, w)


def _ffn_kernel(x_ref, win_ref, wout_ref, g_ref, b_ref, o_ref, *, hid):
    x = x_ref[...]
    xb = x.astype(BF16)
    acc = DEEPNORM_ALPHA * x
    for lo in range(0, hid, FFN_CHUNK):
        w = min(FFN_CHUNK, hid - lo)
        gate = _dot(xb, win_ref[:, lo:lo + w])
        up = _dot(xb, win_ref[:, hid + lo:hid + lo + w])
        mid = (gate * jax.nn.sigmoid(gate) * up).astype(BF16)
        acc = acc + _dot(mid, wout_ref[lo:lo + w, :])
    o_ref[...] = _layer_norm(acc, g_ref[...], b_ref[...])


def _ffn_ln(x, w_in, w_out, g, b, tm=512):
    t, d = x.shape
    hid = w_out.shape[0]
    return pl.pallas_call(
        functools.partial(_ffn_kernel, hid=hid),
        grid=(t // tm,),
        in_specs=[pl.BlockSpec((tm, d), lambda i: (i, 0)),
                  pl.BlockSpec((d, 2 * hid), lambda i: (0, 0), pipeline_mode=pl.Buffered(1)),
                  pl.BlockSpec((hid, d), lambda i: (0, 0), pipeline_mode=pl.Buffered(1)),
                  _const_spec((1, d)), _const_spec((1, d))],
        out_specs=pl.BlockSpec((tm, d), lambda i: (i, 0)),
        out_shape=jax.ShapeDtypeStruct((t, d), F32),
        compiler_params=_params("parallel"),
        name="ffn_ln",
    )(x, w_in.astype(BF16), w_out.astype(BF16), g.reshape(1, d), b.reshape(1, d))


def _s5_tables(lam_re, lam_im, log_step, b_re, b_im, c_re, c_im, nlev):
    L = S5_CHUNK
    lam = lax.complex(lam_re.astype(F32), lam_im.astype(F32))
    step = jnp.exp(log_step.astype(F32))[..., None]
    lam_dt = lam * step
    lam_bar = jnp.exp(lam_dt)
    b_bar = ((lam_bar - 1.0) / lam)[..., None] * lax.complex(b_re.astype(F32), b_im.astype(F32))
    g, p = lam.shape[1], lam.shape[2]
    taus = jnp.arange(S5_PW_ROWS, dtype=F32)
    pw = jnp.exp(lam_dt[:, None] * taus[None, :, None, None])
    pr, pi = jnp.real(pw), jnp.imag(pw)
    cat = lambda a, b: jnp.concatenate([a, b], axis=-1)
    pw_tab = jnp.stack([cat(pr, pr), cat(pi, pi)], axis=1)
    pw_tab = pw_tab.transpose(3, 0, 1, 2, 4).reshape(g, 4, S5_PW_ROWS, 2 * p)
    bt = jnp.swapaxes(b_bar, 2, 3)
    bre, bim = jnp.real(bt), jnp.imag(bt)
    cre, cim = c_re.astype(F32), c_im.astype(F32)
    c2 = cat(cre, -cim)
    bc_tab = jnp.stack([cat(bre, bim), cat(-bim, bre), c2, cat(-cim, -cre)], axis=2)
    bc_tab = bc_tab.transpose(1, 0, 2, 3, 4).reshape(g, 8, S5_GROUP, 2 * p)
    cmat = cat(c2[0], c2[1]).astype(BF16)
    k2 = (L * 2.0 ** jnp.arange(nlev, dtype=F32))
    a = jnp.exp(lam_dt[:, None] * k2[None, :, None, None])
    ar, ai = jnp.real(a), jnp.imag(a)
    p1 = cat(ar, ar).transpose(2, 0, 3, 1).reshape(g, 4 * p, nlev)
    p2 = cat(-ai, ai).transpose(2, 0, 3, 1).reshape(g, 4 * p, nlev)
    return pw_tab, bc_tab, cmat, p1, p2


def _s5_core_kernel(u_ref, pw_ref, bc_ref, cm_ref, p1_ref, p2_ref, y_ref, ut_sc, e_sc, gt_sc, bs_sc, tt_sc,
                    *, nc, nlev):
    L, hh, p, two_p = S5_CHUNK, S5_GROUP, S5_STATE, 2 * S5_STATE
    lh = L * hh
    n = ut_sc.shape[1]

    def blk(d, kind, tau):
        base = 4 * d + 2 * kind
        return (bc_ref[0, base] * pw_ref[0, 2 * d, tau:tau + 1, :]
                + bc_ref[0, base + 1] * pw_ref[0, 2 * d + 1, tau:tau + 1, :])

    zero = jnp.zeros((hh, two_p), BF16)
    for i in range(L):
        rows = slice(i * hh, (i + 1) * hh)
        ut_sc[rows, :] = u_ref[:, i * n:(i + 1) * n]
        e_sc[rows, :two_p] = blk(0, 0, L - 1 - i).astype(BF16)
        e_sc[rows, two_p:] = blk(1, 0, i).astype(BF16)
        gt_sc[rows, :two_p] = blk(0, 1, i + 1).astype(BF16)
        gt_sc[rows, two_p:] = blk(1, 1, L - i).astype(BF16)
    for rho in range(2 * L):
        rows = slice(rho * hh, (rho + 1) * hh)
        lag = L - 1 - rho
        bs_sc[rows, :two_p] = blk(0, 0, lag).astype(BF16) if 0 <= lag < L else zero
        bs_sc[rows, two_p:] = blk(1, 0, -lag).astype(BF16) if 0 <= -lag < L else zero
    strip = _dot_nt(cm_ref[0], bs_sc[...])
    for r in range(128 // hh):
        rot = strip if r == 0 else pltpu.roll(strip, 2 * lh - r * hh, 1)
        for j in range(L):
            a, rr = divmod(L - 1 - j, 128 // hh)
            if rr == r:
                tt_sc[j * hh:(j + 1) * hh, :] = rot[:, 128 * a:128 * a + lh].astype(BF16)

    ut = ut_sc[...]
    y = _dot(tt_sc[...], ut)
    s = _dot_tn(e_sc[...], ut)
    cidx = lax.broadcasted_iota(jnp.int32, (two_p, n), 1) & (nc - 1)

    def cmul(x, k, d):
        a1 = p1_ref[0, d * two_p:(d + 1) * two_p, k:k + 1]
        a2 = p2_ref[0, d * two_p:(d + 1) * two_p, k:k + 1]
        return x * a1 + jnp.concatenate([x[p:], x[:p]], axis=0) * a2

    xf, xb = s[:two_p], s[two_p:]
    for k in range(nlev):
        sh = 1 << k
        xf = xf + jnp.where(cidx >= sh, cmul(pltpu.roll(xf, sh, 1), k, 0), 0.0)
        xb = xb + jnp.where(cidx < nc - sh, cmul(pltpu.roll(xb, n - sh, 1), k, 1), 0.0)
    hf = jnp.where(cidx >= 1, pltpu.roll(xf, 1, 1), 0.0)
    hb = jnp.where(cidx < nc - 1, pltpu.roll(xb, n - 1, 1), 0.0)
    y = y + _dot(gt_sc[...], jnp.concatenate([hf, hb], axis=0).astype(BF16))
    for j in range(L):
        y_ref[:, j * n:(j + 1) * n] = y[j * hh:(j + 1) * hh, :]


def _s5_core(ut, tables, n, nc, nlev):
    d, t = ut.shape
    hh, two_p = S5_GROUP, 2 * S5_STATE
    lh = S5_CHUNK * hh
    blk = lambda a: pl.BlockSpec((1,) + a.shape[1:], lambda i: (i,) + (0,) * (a.ndim - 1))
    return pl.pallas_call(
        functools.partial(_s5_core_kernel, nc=nc, nlev=nlev),
        grid=(d // hh,),
        in_specs=[pl.BlockSpec((hh, t), lambda i: (i, 0))] + [blk(a) for a in tables],
        out_specs=pl.BlockSpec((hh, t), lambda i: (i, 0)),
        out_shape=jax.ShapeDtypeStruct((d, t), F32),
        scratch_shapes=[pltpu.VMEM((lh, n), BF16),
                        pltpu.VMEM((lh, 2 * two_p), BF16), pltpu.VMEM((lh, 2 * two_p), BF16),
                        pltpu.VMEM((2 * lh, 2 * two_p), BF16), pltpu.VMEM((lh, lh), BF16)],
        compiler_params=_params("parallel"),
        name="s5_core",
    )(ut, *tables)


def _s5_in_kernel(x_ref, wT_ref, o_ref):
    o_ref[...] = _dot_nt(wT_ref[...], x_ref[...].astype(BF16)).astype(o_ref.dtype)


def _s5_post_kernel(yT_ref, x_ref, winT_ref, d_ref, wgT_ref, bg_ref, wo_ref, g_ref, b_ref, o_ref):
    x = x_ref[...]
    u = _dot_nt(winT_ref[...], x.astype(BF16))
    y = yT_ref[...] + d_ref[...] * u
    y = jax.nn.gelu(y)
    z = _dot(wgT_ref[...], y.astype(BF16)) + bg_ref[...]
    y = y * jax.nn.sigmoid(z)
    m = _dot_tn(y.astype(BF16), wo_ref[...])
    o_ref[...] = _layer_norm(DEEPNORM_ALPHA * x + m, g_ref[...], b_ref[...])


def _s5_mixer_ln(h, bsz, seq, w_in, lam_re, lam_im, log_step, b_re, b_im, c_re, c_im,
                 d_skip, w_glu, b_glu, w_out, ln_g, ln_b, tm=512):
    t, d = h.shape
    L = S5_CHUNK
    nc = seq // L
    n = bsz * nc
    nlev = max(1, (nc - 1).bit_length())
    assert nc & (nc - 1) == 0
    hp = h.reshape(n, L, d).transpose(1, 0, 2).reshape(t, d)
    w_in_t = w_in.T.astype(BF16)
    tile = pl.BlockSpec((tm, d), lambda i: (i, 0))
    tile_t = pl.BlockSpec((d, tm), lambda i: (0, i))
    ut = pl.pallas_call(
        _s5_in_kernel,
        grid=(t // tm,),
        in_specs=[tile, _const_spec((d, d))],
        out_specs=tile_t,
        out_shape=jax.ShapeDtypeStruct((d, t), BF16),
        compiler_params=_params("parallel"),
        name="s5_in",
    )(hp, w_in_t)
    tables = _s5_tables(lam_re, lam_im, log_step, b_re, b_im, c_re, c_im, nlev)
    yt = _s5_core(ut, tables, n, nc, nlev)
    row = lambda v: v.reshape(1, d)
    col = lambda v: v.astype(F32).reshape(d, 1)
    out = pl.pallas_call(
        _s5_post_kernel,
        grid=(t // tm,),
        in_specs=[tile_t, tile, _const_spec((d, d)), _const_spec((d, 1)), _const_spec((d, d)), _const_spec((d, 1)),
                  _const_spec((d, d)), _const_spec((1, d)), _const_spec((1, d))],
        out_specs=tile,
        out_shape=jax.ShapeDtypeStruct((t, d), F32),
        compiler_params=_params("parallel"),
        name="s5_post",
    )(yt, hp, w_in_t, col(d_skip), w_glu.T.astype(BF16), col(b_glu), w_out.astype(BF16), row(ln_g), row(ln_b))
    return out.reshape(L, n, d).transpose(1, 0, 2).reshape(t, d)


def _hg_masks(c):
    nlev = c.bit_length() - 1
    t = jnp.arange(c)[:, None]
    s = jnp.arange(c)[None, :]
    ms = []
    for lev in range(nlev):
        w = 1 << lev
        ms.append(((t & w) != 0) & ((s & w) == 0) & ((t >> (lev + 1)) == (s >> (lev + 1))))
    ms.append(t == s)
    fw = jnp.stack(ms).astype(F32)
    return jnp.stack([fw, fw.transpose(0, 2, 1)])


def _hg_gate(z, la2, l1m2, onem):
    zs = z * math.log2(math.e)
    ls2 = jnp.minimum(zs, 0.0) - jnp.log2(1.0 + jnp.exp2(-jnp.abs(zs)))
    b2 = l1m2 + ls2
    lf2 = jnp.maximum(la2, b2) + jnp.log2(1.0 + jnp.exp2(-jnp.abs(la2 - b2)))
    return lf2, onem * jnp.exp2(ls2 - zs)


def _hg_in_kernel(x_ref, w_ref, la_ref, l1m_ref, onem_ref, q_ref, lff_ref, lfb_ref, kf_ref, kb_ref, v_ref, g_ref):
    x = x_ref[...].astype(BF16)
    d = q_ref.shape[1]
    pw = HG_IN_PIECE
    npc = d // pw
    piece = lambda n: _dot(x, w_ref[:, n * pw:(n + 1) * pw])

    def finish(n, y):
        kind, cols = n // npc, slice((n % npc) * pw, (n % npc + 1) * pw)
        if kind == 0:
            q_ref[:, cols] = (y * jax.nn.sigmoid(y)).astype(q_ref.dtype)
        elif kind in (1, 2):
            lf_ref, k_ref = (lff_ref, kf_ref) if kind == 1 else (lfb_ref, kb_ref)
            lf_ref[:, cols], k = _hg_gate(y, la_ref[:, cols], l1m_ref[:, cols], onem_ref[:, cols])
            k_ref[:, cols] = k.astype(k_ref.dtype)
        elif kind == 3:
            v_ref[:, cols] = y.astype(v_ref.dtype)
        else:
            g_ref[:, cols] = y

    y = piece(0)
    for n in range(5 * npc):
        y_next = piece(n + 1) if n + 1 < 5 * npc else None
        finish(n, y)
        y = y_next


def _hg_chunk(qb, lf, kb, vb, mask_ref, d, st_ref, emit):
    c = qb.shape[0]
    nlev = c.bit_length() - 1
    row = lax.broadcasted_iota(jnp.int32, lf.shape, 0)
    scores = mask_ref[d, nlev] * _dot_nt(qb, kb)
    p, tot = (lf if d == 0 else jnp.zeros_like(lf)), lf
    yield
    for lev in range(min(nlev, 3)):
        w = 1 << lev
        odd = (row & w) != 0
        f = jnp.exp2(jnp.where(odd, p, tot - p)).astype(BF16)
        scores = scores + mask_ref[d, lev] * _dot_nt(qb * f, kb * f)
        up = pltpu.roll(tot, w, 0)
        dn = pltpu.roll(tot, c - w, 0)
        p = p + jnp.where(odd, up, 0.0)
        tot = tot + jnp.where(odd, up, dn)
        yield
    for lev in range(3, nlev):
        w = 1 << lev
        nb = c // w
        cut = lambda x: [x[i * w:(i + 1) * w] for i in range(nb)]
        cat = lambda xs: jnp.concatenate(xs, axis=0)
        pb, tb, sb = cut(p), cut(tot), cut(scores)
        f = jnp.exp2(cat([pb[i] if i % 2 else tb[i] - pb[i] for i in range(nb)])).astype(BF16)
        q_side = lambda i: (i % 2 == 1) == (d == 0)
        if w >= 16:
            x = cat([a if q_side(i) else b for i, (a, b) in enumerate(zip(cut(qb), cut(kb)))]) * f
            dots = _dot_nt(x, x)
        else:
            dots = _dot_nt(qb * f, kb * f)
        db = cut(dots)
        scores = cat([sb[i] + mask_ref[d, lev, i * w:(i + 1) * w, :] * db[i] if q_side(i) else sb[i]
                      for i in range(nb)])
        p = cat([pb[i] + tb[i - 1] if i % 2 else pb[i] for i in range(nb)])
        tot = cat([tb[i] + tb[i ^ 1] for i in range(nb)])
        yield
    eq, ek = (p, tot - p) if d == 0 else (tot - p, p)
    st = st_ref[...]
    o = _dot(scores.astype(BF16), vb)
    o = o + _dot_nt(qb * jnp.exp2(eq).astype(BF16), st.astype(BF16))
    st_ref[...] = st * jnp.exp2(tot[0:1, :]) + _dot_tn(vb, kb * jnp.exp2(ek).astype(BF16))
    emit(o)


def _round_robin(gens):
    gens = list(gens)
    while gens:
        gens = [g for g in gens if next(g, StopIteration) is not StopIteration]


def _hg_rec_kernel(qf_ref, lff_ref, kf_ref, vf_ref, qb_ref, lfb_ref, kb_ref, vb_ref, mask_ref,
                   of_ref, ob_ref, stf_ref, stb_ref, *, nsub):
    @pl.when(pl.program_id(2) == 0)
    def _():
        stf_ref[...] = jnp.zeros_like(stf_ref)
        stb_ref[...] = jnp.zeros_like(stb_ref)

    c = HG_CHUNK

    def chain(d, rows):
        q_ref, lf_ref, k_ref, v_ref, o_ref, st_ref = ((qf_ref, lff_ref, kf_ref, vf_ref, of_ref, stf_ref) if d == 0 else
                                                      (qb_ref, lfb_ref, kb_ref, vb_ref, ob_ref, stb_ref))

        def emit(o):
            o_ref[rows, :] = o
        return _hg_chunk(q_ref[rows, :], lf_ref[rows, :], k_ref[rows, :], v_ref[rows, :], mask_ref, d, st_ref, emit)

    for j0 in range(0, nsub, HG_INTERLEAVE):
        gens = []
        for j in range(j0, min(j0 + HG_INTERLEAVE, nsub)):
            gens.append(chain(0, slice(j * c, (j + 1) * c)))
            gens.append(chain(1, slice((nsub - 1 - j) * c, (nsub - j) * c)))
        _round_robin(gens)


def _hg_post_kernel(of_ref, ob_ref, g_ref, h_ref, ng_ref, wo_ref, lg_ref, lb_ref, o_ref):
    o = of_ref[...] + ob_ref[...]
    hd = HG_HEAD_DIM
    parts = []
    for i in range(o.shape[1] // hd):
        oh = o[:, i * hd:(i + 1) * hd]
        parts.append(oh * lax.rsqrt(jnp.mean(oh * oh, axis=-1, keepdims=True) + RMS_EPS))
    on = jnp.concatenate(parts, axis=1) * ng_ref[...]
    gate = g_ref[...]
    y = on * (gate * jax.nn.sigmoid(gate))
    m = _dot(y.astype(BF16), wo_ref[...])
    o_ref[...] = _layer_norm(DEEPNORM_ALPHA * h_ref[...] + m, lg_ref[...], lb_ref[...])


def _hg_mixer_ln(h, bsz, seq, w_in, lb, norm_g, w_out, ln_g, ln_b, tm=512):
    t, d = h.shape
    hd = HG_HEAD_DIM
    nh = d // hd
    blk = min(HG_BLOCK, seq)
    nblk = seq // blk
    lb = lb.astype(F32).reshape(1, d)
    la, l1m, onem = jnp.log2(lb), jnp.log1p(-lb) * math.log2(math.e), 1.0 - lb
    row = lambda v: v.reshape(1, d)
    tile = pl.BlockSpec((tm, d), lambda i: (i, 0))
    act = lambda dt: jax.ShapeDtypeStruct((t, d), dt)
    qa, lff, lfb, kf, kb, v, gate = pl.pallas_call(
        _hg_in_kernel,
        grid=(t // tm,),
        in_specs=[tile, pl.BlockSpec(w_in.shape, lambda i: (0, 0), pipeline_mode=pl.Buffered(1)),
                  _const_spec((1, d)), _const_spec((1, d)), _const_spec((1, d))],
        out_specs=[tile] * 7,
        out_shape=[act(BF16), act(F32), act(F32), act(BF16), act(BF16), act(BF16), act(F32)],
        compiler_params=_params("parallel"),
        name="hg_in",
    )(h, w_in.astype(BF16), la, l1m, onem)
    masks = _hg_masks(HG_CHUNK)
    fw = pl.BlockSpec((blk, hd), lambda b, hh, i: (b * nblk + i, hh))
    bw = pl.BlockSpec((blk, hd), lambda b, hh, i: (b * nblk + nblk - 1 - i, hh))
    o_fw, o_bw = pl.pallas_call(
        functools.partial(_hg_rec_kernel, nsub=blk // HG_CHUNK),
        grid=(bsz, nh, nblk),
        in_specs=[fw, fw, fw, fw, bw, bw, bw, bw, _const_spec(masks.shape)],
        out_specs=[fw, bw],
        out_shape=[act(F32)] * 2,
        scratch_shapes=[pltpu.VMEM((hd, hd), F32), pltpu.VMEM((hd, hd), F32)],
        compiler_params=_params("parallel", "parallel", "arbitrary"),
        name="hg_rec",
    )(qa, lff, kf, v, qa, lfb, kb, v, masks)

    return pl.pallas_call(
        _hg_post_kernel,
        grid=(t // tm,),
        in_specs=[tile, tile, tile, tile, _const_spec((1, d)),
                  _const_spec((d, d)), _const_spec((1, d)), _const_spec((1, d))],
        out_specs=tile,
        out_shape=act(F32),
        compiler_params=_params("parallel"),
        name="hg_post",
    )(o_fw, o_bw, gate, h, jnp.tile(norm_g.astype(F32), nh).reshape(1, d), w_out.astype(BF16), row(ln_g), row(ln_b))


def _mla_proj_kernel(x_ref, pos_ref, winT_ref, gq_ref, gkv_ref, wqT_ref, wkvT_ref, invf_ref,
                     qT_ref, k_ref, vT_ref):
    ql, kvl, half = MLA_Q_LORA, MLA_KV_LORA, MLA_ROPE // 2
    dq = MLA_NOPE + MLA_ROPE
    lat = _dot_nt(winT_ref[...], x_ref[...].astype(BF16))
    q_lat, kv_lat, k_rope = lat[:ql], lat[ql:ql + kvl], lat[ql + kvl:]

    def rms(v, g):
        return (v * lax.rsqrt(jnp.mean(v * v, axis=0, keepdims=True) + RMS_EPS) * g).astype(BF16)

    q = _dot(wqT_ref[...], rms(q_lat, gq_ref[...]))
    kv = _dot(wkvT_ref[...], rms(kv_lat, gkv_ref[...]))
    ang = invf_ref[...] * pos_ref[0].astype(F32)
    cos, sin = jnp.cos(ang), jnp.sin(ang)

    def rope(t1, t2):
        return t1 * cos - t2 * sin, t1 * sin + t2 * cos

    scale = dq ** -0.5 * math.log2(math.e)
    ones_rows = (lax.broadcasted_iota(jnp.int32, (16, lat.shape[1]), 0) == 0).astype(BF16)
    k1, k2 = rope(k_rope[:half], k_rope[half:])
    pad = jnp.zeros((MLA_QK_PAD - dq, lat.shape[1]), F32)
    k_tail = jnp.concatenate([k1, k2, pad], axis=0).T.astype(BF16)
    for hh in range(MLA_HEADS):
        qh = q[hh * dq:(hh + 1) * dq]
        q1, q2 = rope(qh[MLA_NOPE:MLA_NOPE + half], qh[MLA_NOPE + half:])
        qT_ref[0, hh, 0:MLA_NOPE, :] = (qh[:MLA_NOPE] * scale).astype(BF16)
        qT_ref[0, hh, MLA_NOPE:MLA_NOPE + half, :] = (q1 * scale).astype(BF16)
        qT_ref[0, hh, MLA_NOPE + half:dq, :] = (q2 * scale).astype(BF16)
        qT_ref[0, hh, dq:, :] = pad.astype(BF16)
        kvh = kv[hh * (MLA_NOPE + MLA_V):(hh + 1) * (MLA_NOPE + MLA_V)]
        k_ref[0, hh, 0, :, 0:MLA_NOPE] = kvh[:MLA_NOPE].T.astype(BF16)
        k_ref[0, hh, 0, :, MLA_NOPE:] = k_tail
        vT_ref[0, hh, 0, 0:MLA_V, :] = kvh[MLA_NOPE:].astype(BF16)
        vT_ref[0, hh, 0, MLA_V:, :] = ones_rows


def _attn_kernel(qT_ref, k_ref, vT_ref, o_ref, s0_sc, s1_sc, t0_sc, t1_sc, m_sc, acc_sc, *, nkv, tqs, tks):
    dv = MLA_V
    s_sc, t_sc = (s0_sc, s1_sc), (t0_sc, t1_sc)
    tk, tq = s0_sc.shape
    m_sc[...] = jnp.full_like(m_sc, -jnp.inf)
    acc_sc[...] = jnp.zeros_like(acc_sc)

    def scores(j, par, c):
        cols = slice(c * tqs, (c + 1) * tqs)
        s = _dot(k_ref[0, 0, j], qT_ref[0, 0, :, cols])
        s_sc[par][:, cols] = s
        t_sc[par][:, cols] = jnp.max(s, axis=0, keepdims=True)

    def softmax_pv(j, par, c):
        cols = slice(c * tqs, (c + 1) * tqs)
        m_prev = m_sc[:, cols]
        m_new = jnp.maximum(m_prev, t_sc[par][:, cols])
        acc = jnp.exp2(m_prev - m_new) * acc_sc[:, cols]
        for r in range(tk // tks):
            rows = slice(r * tks, (r + 1) * tks)
            p = jnp.exp2(s_sc[par][rows, cols] - m_new).astype(BF16)
            acc = acc + _dot(vT_ref[0, 0, j, :, rows], p)
        acc_sc[:, cols] = acc
        m_sc[:, cols] = m_new

    nsub = tq // tqs

    def step(j_scores, j_soft):
        for c in range(nsub):
            if j_scores is not None:
                scores(j_scores[0], j_scores[1], c)
            if j_soft is not None:
                softmax_pv(j_soft[0], j_soft[1], c)

    step((0, 0), None)

    def body(jj, carry):
        j = 2 * jj
        step((j + 1, 1), (j, 0))
        step((j + 2, 0), (j + 1, 1))
        return carry

    lax.fori_loop(0, nkv // 2 - 1, body, 0, unroll=3)
    step((nkv - 1, 1), (nkv - 2, 0))
    step(None, (nkv - 1, 1))
    acc = acc_sc[...]
    o_ref[0, 0] = (acc[:dv] / acc[dv:dv + 1]).astype(o_ref.dtype)


def _mla_out_kernel(oT_ref, h_ref, wo_ref, g_ref, b_ref, o_ref):
    oT = oT_ref[0]
    oT = oT.reshape(oT.shape[0] * oT.shape[1], oT.shape[2])
    m = _dot_tn(oT, wo_ref[...])
    o_ref[...] = _layer_norm(DEEPNORM_ALPHA * h_ref[...] + m, g_ref[...], b_ref[...])


def _mla_mixer_ln(h, positions, bsz, seq, w_in, q_norm_g, w_q_b, kv_norm_g, w_kv_b, w_out, ln_g, ln_b,
                  tm=512, tq=4096, tqs=256, tks=256):
    t, d = h.shape
    nh, dq, dv = MLA_HEADS, MLA_NOPE + MLA_ROPE, MLA_V
    half = MLA_ROPE // 2
    inv_freq = 1.0 / (ROPE_THETA ** (jnp.arange(half, dtype=F32) * (2.0 / MLA_ROPE)))
    nlat = w_in.shape[1]
    nt = seq // tm
    dp = MLA_QK_PAD
    qT, kc, vT = pl.pallas_call(
        _mla_proj_kernel,
        grid=(bsz, nt),
        in_specs=[pl.BlockSpec((tm, d), lambda b, i: (b * nt + i, 0)),
                  pl.BlockSpec((1, 1, tm), lambda b, i: (b, 0, i)),
                  _const_spec((nlat, d)), _const_spec((MLA_Q_LORA, 1)), _const_spec((MLA_KV_LORA, 1)),
                  _const_spec((nh * dq, MLA_Q_LORA)), _const_spec((nh * (MLA_NOPE + dv), MLA_KV_LORA)),
                  _const_spec((half, 1))],
        out_specs=[pl.BlockSpec((1, nh, dp, tm), lambda b, i: (b, 0, 0, i)),
                   pl.BlockSpec((1, nh, 1, tm, dp), lambda b, i: (b, 0, i, 0, 0)),
                   pl.BlockSpec((1, nh, 1, dv + 16, tm), lambda b, i: (b, 0, i, 0, 0))],
        out_shape=[jax.ShapeDtypeStruct((bsz, nh, dp, seq), BF16),
                   jax.ShapeDtypeStruct((bsz, nh, nt, tm, dp), BF16),
                   jax.ShapeDtypeStruct((bsz, nh, nt, dv + 16, tm), BF16)],
        compiler_params=_params("parallel", "parallel"),
        name="mla_proj",
    )(h, positions.reshape(bsz, 1, seq), w_in.T.astype(BF16), q_norm_g.astype(F32).reshape(-1, 1),
      kv_norm_g.astype(F32).reshape(-1, 1), w_q_b.T.astype(BF16), w_kv_b.T.astype(BF16),
      inv_freq.reshape(half, 1))

    tk, nkv = tm, nt
    tq = min(tq, seq)
    oT = pl.pallas_call(
        functools.partial(_attn_kernel, nkv=nkv, tqs=tqs, tks=tks),
        grid=(bsz, nh, seq // tq),
        in_specs=[pl.BlockSpec((1, 1, dp, tq), lambda b, hh, i: (b, hh, 0, i)),
                  pl.BlockSpec((1, 1, nkv, tk, dp), lambda b, hh, i: (b, hh, 0, 0, 0),
                               pipeline_mode=pl.Buffered(1)),
                  pl.BlockSpec((1, 1, nkv, dv + 16, tk), lambda b, hh, i: (b, hh, 0, 0, 0),
                               pipeline_mode=pl.Buffered(1))],
        out_specs=pl.BlockSpec((1, 1, dv, tq), lambda b, hh, i: (b, hh, 0, i)),
        out_shape=jax.ShapeDtypeStruct((bsz, nh, dv, seq), BF16),
        scratch_shapes=[pltpu.VMEM((tk, tq), F32), pltpu.VMEM((tk, tq), F32),
                        pltpu.VMEM((1, tq), F32), pltpu.VMEM((1, tq), F32),
                        pltpu.VMEM((1, tq), F32), pltpu.VMEM((dv + 16, tq), F32)],
        compiler_params=_params("parallel", "parallel", "arbitrary"),
        name="mla_attn",
    )(qT, kc, vT)

    return pl.pallas_call(
        _mla_out_kernel,
        grid=(bsz, nt),
        in_specs=[pl.BlockSpec((1, nh, dv, tm), lambda b, i: (b, 0, 0, i)),
                  pl.BlockSpec((tm, d), lambda b, i: (b * nt + i, 0)),
                  _const_spec((nh * dv, d)), _const_spec((1, d)), _const_spec((1, d))],
        out_specs=pl.BlockSpec((tm, d), lambda b, i: (b * nt + i, 0)),
        out_shape=jax.ShapeDtypeStruct((t, d), F32),
        compiler_params=_params("parallel", "parallel"),
        name="mla_out",
    )(oT, h, w_out.astype(BF16), ln_g.reshape(1, d), ln_b.reshape(1, d))


def kernel(x, positions, ln_mix_g, ln_mix_b, ln_ffn_g, ln_ffn_b, ffn_w_in, ffn_w_out, s5_w_in, s5_lam_re, s5_lam_im, s5_log_step, s5_b_re, s5_b_im, s5_c_re, s5_c_im, s5_d, s5_w_glu, s5_b_glu, s5_w_out, hg_w_in, hg_lower_bound, hg_norm_g, hg_w_out, mla_w_in, mla_q_norm_g, mla_w_q_b, mla_kv_norm_g, mla_w_kv_b, mla_w_out):
    bsz, seq, d = x.shape
    lbs = jax.nn.softmax(hg_lower_bound.astype(F32), axis=0)
    lbs = jnp.cumsum(lbs, axis=0) - lbs[0]
    h = x.reshape(bsz * seq, d)
    for layer in range(DEPTH):
        kind = layer % N_MIXERS
        slot = layer // N_MIXERS
        lg, lb = ln_mix_g[layer], ln_mix_b[layer]
        if kind == 0:
            h = _s5_mixer_ln(h, bsz, seq, s5_w_in[slot], s5_lam_re[slot], s5_lam_im[slot], s5_log_step[slot],
                             s5_b_re[slot], s5_b_im[slot], s5_c_re[slot], s5_c_im[slot], s5_d[slot],
                             s5_w_glu[slot], s5_b_glu[slot], s5_w_out[slot], lg, lb)
        elif kind == 1:
            h = _hg_mixer_ln(h, bsz, seq, hg_w_in[slot], lbs[layer], hg_norm_g[slot], hg_w_out[slot], lg, lb)
        else:
            h = _mla_mixer_ln(h, positions, bsz, seq, mla_w_in[slot], mla_q_norm_g[slot], mla_w_q_b[slot],
                              mla_kv_norm_g[slot], mla_w_kv_b[slot], mla_w_out[slot], lg, lb)
        h = _ffn_ln(h, ffn_w_in[layer], ffn_w_out[layer], ln_ffn_g[layer], ln_ffn_b[layer])
    return h.reshape(bsz, seq, d)
```

```python
import functools
import math

import jax
import jax.numpy as jnp
from jax import lax
from jax.experimental import pallas as pl
from jax.experimental.pallas import tpu as pltpu

F32 = jnp.float32
BF16 = jnp.bfloat16
LANES = 128
BF16_ROWS = 16

DEPTH = 4
N_MIXERS = 3
S5_GROUP = 16
S5_STATE = 64
S5_CHUNK = 64
S5_POST_SUB = 256
S5_PW_ROWS = 72
HG_HEAD_DIM = 128
HG_CHUNK = 128
HG_BLOCK = 2048
HG_IN_PIECE = (256, 256)
HG_IN_AHEAD = 2
MLA_HEADS = 8
MLA_NOPE = 128
MLA_ROPE = 64
MLA_V = 128
MLA_QK_PAD = 256
MLA_Q_LORA = 384
MLA_KV_LORA = 256
ROPE_THETA = 10000.0
DEEPNORM_ALPHA = (2 * DEPTH) ** 0.25
LN_EPS = 1e-5
RMS_EPS = 1e-6
FFN_CHUNK = 512
VMEM_LIMIT = 56 * 1024 * 1024


def _params(*sem):
    return pltpu.CompilerParams(dimension_semantics=sem, vmem_limit_bytes=VMEM_LIMIT)


def _dot(a, b):
    return jnp.dot(a, b, preferred_element_type=F32)


def _dot_nt(a, b):
    return lax.dot_general(a, b, (((1,), (1,)), ((), ())), preferred_element_type=F32)


def _dot_tn(a, b):
    return lax.dot_general(a, b, (((0,), (0,)), ((), ())), preferred_element_type=F32)


def _layer_norm(y, g, b):
    mu = jnp.mean(y, axis=-1, keepdims=True)
    d = y - mu
    var = jnp.mean(d * d, axis=-1, keepdims=True)
    return d * lax.rsqrt(var + LN_EPS) * g + b


def _const_spec(shape):
    return pl.BlockSpec(shape, lambda *_: (0,) * len(shape))


def _round_robin(gens, stagger=0):
    live = list(enumerate(gens))
    rnd = 0
    while live:
        live = [(i, g) for i, g in live if rnd < i * stagger or next(g, StopIteration) is not StopIteration]
        rnd += 1


def _ffn_kernel(x_ref, win_ref, wout_ref, g_ref, b_ref, o_ref, *, hid):
    x = x_ref[...]
    xb = x.astype(BF16)
    acc = DEEPNORM_ALPHA * x
    for lo in range(0, hid, FFN_CHUNK):
        w = min(FFN_CHUNK, hid - lo)
        gate = _dot(xb, win_ref[:, lo:lo + w])
        up = _dot(xb, win_ref[:, hid + lo:hid + lo + w])
        mid = (gate * jax.nn.sigmoid(gate) * up).astype(BF16)
        acc = acc + _dot(mid, wout_ref[lo:lo + w, :])
    o_ref[...] = _layer_norm(acc, g_ref[...], b_ref[...])


def _ffn_ln(x, w_in, w_out, g, b, tm=1024):
    t, d = x.shape
    hid = w_out.shape[0]
    return pl.pallas_call(
        functools.partial(_ffn_kernel, hid=hid),
        grid=(t // tm,),
        in_specs=[pl.BlockSpec((tm, d), lambda i: (i, 0)),
                  pl.BlockSpec((d, 2 * hid), lambda i: (0, 0), pipeline_mode=pl.Buffered(1)),
                  pl.BlockSpec((hid, d), lambda i: (0, 0), pipeline_mode=pl.Buffered(1)),
                  _const_spec((1, d)), _const_spec((1, d))],
        out_specs=pl.BlockSpec((tm, d), lambda i: (i, 0)),
        out_shape=jax.ShapeDtypeStruct((t, d), F32),
        compiler_params=_params("parallel"),
        name="ffn_ln",
    )(x, w_in.astype(BF16), w_out.astype(BF16), g.reshape(1, d), b.reshape(1, d))


def _s5_tables(lam_re, lam_im, log_step, b_re, b_im, c_re, c_im, nlev):
    L = S5_CHUNK
    lam = lax.complex(lam_re.astype(F32), lam_im.astype(F32))
    step = jnp.exp(log_step.astype(F32))[..., None]
    lam_dt = lam * step
    lam_bar = jnp.exp(lam_dt)
    b_bar = ((lam_bar - 1.0) / lam)[..., None] * lax.complex(b_re.astype(F32), b_im.astype(F32))
    g, p = lam.shape[1], lam.shape[2]
    taus = jnp.arange(S5_PW_ROWS, dtype=F32)
    pw = jnp.exp(lam_dt[:, None] * taus[None, :, None, None])
    pr, pi = jnp.real(pw), jnp.imag(pw)
    cat = lambda a, b: jnp.concatenate([a, b], axis=-1)
    pw_tab = jnp.stack([cat(pr, pr), cat(pi, pi)], axis=1)
    pw_tab = pw_tab.transpose(3, 0, 1, 2, 4).reshape(g, 4, S5_PW_ROWS, 2 * p)
    bt = jnp.swapaxes(b_bar, 2, 3)
    bre, bim = jnp.real(bt), jnp.imag(bt)
    cre, cim = c_re.astype(F32), c_im.astype(F32)
    c2 = cat(cre, -cim)
    bc_tab = jnp.stack([cat(bre, bim), cat(-bim, bre), c2, cat(-cim, -cre)], axis=2)
    bc_tab = bc_tab.transpose(1, 0, 2, 3, 4).reshape(g, 8, S5_GROUP, 2 * p)
    cmat = cat(c2[0], c2[1]).astype(BF16)
    k2 = (L * 2.0 ** jnp.arange(nlev, dtype=F32))
    a = jnp.exp(lam_dt[:, None] * k2[None, :, None, None])
    ar, ai = jnp.real(a), jnp.imag(a)
    p1 = cat(ar, ar).transpose(2, 0, 3, 1).reshape(g, 4 * p, nlev)
    p2 = cat(-ai, ai).transpose(2, 0, 3, 1).reshape(g, 4 * p, nlev)
    return pw_tab, bc_tab, cmat, p1, p2


def _s5_core_kernel(u_ref, pw_ref, bc_ref, cm_ref, p1_ref, p2_ref, y_ref, ut_sc, e_sc, gt_sc, bs_sc, tt_sc,
                    *, nc, nlev):
    L, hh, p, two_p = S5_CHUNK, S5_GROUP, S5_STATE, 2 * S5_STATE
    lh = L * hh
    n = ut_sc.shape[1]

    def blk(d, kind, tau):
        base = 4 * d + 2 * kind
        return (bc_ref[0, base] * pw_ref[0, 2 * d, tau:tau + 1, :]
                + bc_ref[0, base + 1] * pw_ref[0, 2 * d + 1, tau:tau + 1, :])

    zero = jnp.zeros((hh, two_p), BF16)
    for i in range(L):
        rows = slice(i * hh, (i + 1) * hh)
        ut_sc[rows, :] = u_ref[:, i * n:(i + 1) * n]
        e_sc[rows, :two_p] = blk(0, 0, L - 1 - i).astype(BF16)
        e_sc[rows, two_p:] = blk(1, 0, i).astype(BF16)
        gt_sc[rows, :two_p] = blk(0, 1, i + 1).astype(BF16)
        gt_sc[rows, two_p:] = blk(1, 1, L - i).astype(BF16)
    for rho in range(2 * L):
        rows = slice(rho * hh, (rho + 1) * hh)
        lag = L - 1 - rho
        bs_sc[rows, :two_p] = blk(0, 0, lag).astype(BF16) if 0 <= lag < L else zero
        bs_sc[rows, two_p:] = blk(1, 0, -lag).astype(BF16) if 0 <= -lag < L else zero
    strip = _dot_nt(cm_ref[0], bs_sc[...])
    for r in range(LANES // hh):
        rot = strip if r == 0 else pltpu.roll(strip, 2 * lh - r * hh, 1)
        for j in range(L):
            a, rr = divmod(L - 1 - j, LANES // hh)
            if rr == r:
                tt_sc[j * hh:(j + 1) * hh, :] = rot[:, LANES * a:LANES * a + lh].astype(BF16)

    ut = ut_sc[...]
    y = _dot(tt_sc[...], ut)
    s = _dot_tn(e_sc[...], ut)
    cidx = lax.broadcasted_iota(jnp.int32, (two_p, n), 1) & (nc - 1)

    def cmul(x, k, d):
        a1 = p1_ref[0, d * two_p:(d + 1) * two_p, k:k + 1]
        a2 = p2_ref[0, d * two_p:(d + 1) * two_p, k:k + 1]
        return x * a1 + jnp.concatenate([x[p:], x[:p]], axis=0) * a2

    xf, xb = s[:two_p], s[two_p:]
    for k in range(nlev):
        sh = 1 << k
        xf = xf + jnp.where(cidx >= sh, cmul(pltpu.roll(xf, sh, 1), k, 0), 0.0)
        xb = xb + jnp.where(cidx < nc - sh, cmul(pltpu.roll(xb, n - sh, 1), k, 1), 0.0)
    hf = jnp.where(cidx >= 1, pltpu.roll(xf, 1, 1), 0.0)
    hb = jnp.where(cidx < nc - 1, pltpu.roll(xb, n - 1, 1), 0.0)
    y = y + _dot(gt_sc[...], jnp.concatenate([hf, hb], axis=0).astype(BF16))
    for j in range(L):
        y_ref[:, j * n:(j + 1) * n] = y[j * hh:(j + 1) * hh, :]


def _s5_core(ut, tables, n, nc, nlev):
    d, t = ut.shape
    hh, two_p = S5_GROUP, 2 * S5_STATE
    lh = S5_CHUNK * hh
    blk = lambda a: pl.BlockSpec((1,) + a.shape[1:], lambda i: (i,) + (0,) * (a.ndim - 1))
    return pl.pallas_call(
        functools.partial(_s5_core_kernel, nc=nc, nlev=nlev),
        grid=(d // hh,),
        in_specs=[pl.BlockSpec((hh, t), lambda i: (i, 0))] + [blk(a) for a in tables],
        out_specs=pl.BlockSpec((hh, t), lambda i: (i, 0)),
        out_shape=jax.ShapeDtypeStruct((d, t), F32),
        scratch_shapes=[pltpu.VMEM((lh, n), BF16),
                        pltpu.VMEM((lh, 2 * two_p), BF16), pltpu.VMEM((lh, 2 * two_p), BF16),
                        pltpu.VMEM((2 * lh, 2 * two_p), BF16), pltpu.VMEM((lh, lh), BF16)],
        compiler_params=_params("parallel"),
        name="s5_core",
    )(ut, *tables)


def _s5_in_kernel(x_ref, wT_ref, o_ref):
    o_ref[...] = _dot_nt(wT_ref[...], x_ref[...].astype(BF16)).astype(o_ref.dtype)


def _s5_post_kernel(yT_ref, x_ref, winT_ref, d_ref, wgT_ref, bg_ref, wo_ref, g_ref, b_ref, o_ref):
    ts = S5_POST_SUB

    def sub_tile(i):
        tok = slice(i * ts, (i + 1) * ts)
        x = x_ref[tok, :]
        u = _dot_nt(winT_ref[...], x.astype(BF16))
        yield
        y = jax.nn.gelu(yT_ref[:, tok] + d_ref[...] * u)
        yield
        z = _dot(wgT_ref[...], y.astype(BF16)) + bg_ref[...]
        yield
        y = y * jax.nn.sigmoid(z)
        yield
        m = _dot_tn(y.astype(BF16), wo_ref[...])
        yield
        o_ref[tok, :] = _layer_norm(DEEPNORM_ALPHA * x + m, g_ref[...], b_ref[...])

    _round_robin([sub_tile(i) for i in range(x_ref.shape[0] // ts)], stagger=1)


def _s5_mixer_ln(h, bsz, seq, w_in, lam_re, lam_im, log_step, b_re, b_im, c_re, c_im,
                 d_skip, w_glu, b_glu, w_out, ln_g, ln_b, tm=1024):
    t, d = h.shape
    L = S5_CHUNK
    nc = seq // L
    n = bsz * nc
    nlev = max(1, (nc - 1).bit_length())
    assert nc & (nc - 1) == 0
    hp = h.reshape(n, L, d).transpose(1, 0, 2).reshape(t, d)
    w_in_t = w_in.T.astype(BF16)
    tile = pl.BlockSpec((tm, d), lambda i: (i, 0))
    tile_t = pl.BlockSpec((d, tm), lambda i: (0, i))
    ut = pl.pallas_call(
        _s5_in_kernel,
        grid=(t // tm,),
        in_specs=[tile, _const_spec((d, d))],
        out_specs=tile_t,
        out_shape=jax.ShapeDtypeStruct((d, t), BF16),
        compiler_params=_params("parallel"),
        name="s5_in",
    )(hp, w_in_t)
    tables = _s5_tables(lam_re, lam_im, log_step, b_re, b_im, c_re, c_im, nlev)
    yt = _s5_core(ut, tables, n, nc, nlev)
    row = lambda v: v.reshape(1, d)
    col = lambda v: v.astype(F32).reshape(d, 1)
    out = pl.pallas_call(
        _s5_post_kernel,
        grid=(t // tm,),
        in_specs=[tile_t, tile, _const_spec((d, d)), _const_spec((d, 1)), _const_spec((d, d)), _const_spec((d, 1)),
                  _const_spec((d, d)), _const_spec((1, d)), _const_spec((1, d))],
        out_specs=tile,
        out_shape=jax.ShapeDtypeStruct((t, d), F32),
        compiler_params=_params("parallel"),
        name="s5_post",
    )(yt, hp, w_in_t, col(d_skip), w_glu.T.astype(BF16), col(b_glu), w_out.astype(BF16), row(ln_g), row(ln_b))
    return out.reshape(L, n, d).transpose(1, 0, 2).reshape(t, d)


def _hg_masks(c):
    nlev = c.bit_length() - 1
    t = jnp.arange(c)[:, None]
    s = jnp.arange(c)[None, :]
    ms = []
    for lev in range(nlev):
        w = 1 << lev
        ms.append(((t & w) != 0) & ((s & w) == 0) & ((t >> (lev + 1)) == (s >> (lev + 1))))
    ms.append(t == s)
    fw = jnp.stack(ms).astype(F32)
    return jnp.stack([fw, fw.transpose(0, 2, 1)])


def _hg_gate(z, la2, l1m2, onem):
    zs = z * math.log2(math.e)
    ls2 = jnp.minimum(zs, 0.0) - jnp.log2(1.0 + jnp.exp2(-jnp.abs(zs)))
    b2 = l1m2 + ls2
    lf2 = jnp.maximum(la2, b2) + jnp.log2(1.0 + jnp.exp2(-jnp.abs(la2 - b2)))
    return lf2, onem * jnp.exp2(ls2 - zs)


def _hg_in_kernel(x_ref, w_ref, la_ref, l1m_ref, onem_ref, q_ref, lff_ref, lfb_ref, kf_ref, kb_ref, v_ref, g_ref,
                  y_sc):
    d = q_ref.shape[1]
    pw, ph = HG_IN_PIECE
    npc = d // pw
    nrb = x_ref.shape[0] // ph
    xs = [x_ref[r * ph:(r + 1) * ph, :].astype(BF16) for r in range(nrb)]
    piece = lambda n, r: _dot(xs[r], w_ref[:, n * pw:(n + 1) * pw])

    def finish(n, r, y):
        kind, cols, rows = n // npc, slice((n % npc) * pw, (n % npc + 1) * pw), slice(r * ph, (r + 1) * ph)
        if kind == 0:
            q_ref[rows, cols] = (y * jax.nn.sigmoid(y)).astype(q_ref.dtype)
        elif kind in (1, 2):
            lf_ref, k_ref = (lff_ref, kf_ref) if kind == 1 else (lfb_ref, kb_ref)
            lf_ref[rows, cols], k = _hg_gate(y, la_ref[:, cols], l1m_ref[:, cols], onem_ref[:, cols])
            k_ref[rows, cols] = k.astype(k_ref.dtype)
        elif kind == 3:
            v_ref[rows, cols] = y.astype(v_ref.dtype)
        else:
            g_ref[rows, cols] = y

    heavy = [(n, r) for n in range(5 * npc) if n // npc in (1, 2) for r in range(nrb)]
    light = [(n, r) for n in range(5 * npc) if n // npc not in (1, 2) for r in range(nrb)]
    order = []
    while heavy or light:
        order += heavy[:1] + light[:1]
        heavy, light = heavy[1:], light[1:]
    nslot = y_sc.shape[0]
    for i in range(len(order) + HG_IN_AHEAD):
        if i < len(order):
            y_sc[i % nslot] = piece(*order[i])
        if i >= HG_IN_AHEAD:
            finish(*order[i - HG_IN_AHEAD], y_sc[(i - HG_IN_AHEAD) % nslot])


def _hg_chunk(qb, lf, kb, vb, mask_ref, d, st_ref, emit):
    c = qb.shape[0]
    nlev = c.bit_length() - 1
    row = lax.broadcasted_iota(jnp.int32, lf.shape, 0)
    scores = mask_ref[d, nlev] * _dot_nt(qb, kb)
    p, tot = (lf if d == 0 else jnp.zeros_like(lf)), lf
    yield
    for lev in range(min(nlev, 3)):
        w = 1 << lev
        odd = (row & w) != 0
        f = jnp.exp2(jnp.where(odd, p, tot - p)).astype(BF16)
        scores = scores + mask_ref[d, lev] * _dot_nt(qb * f, kb * f)
        up = pltpu.roll(tot, w, 0)
        dn = pltpu.roll(tot, c - w, 0)
        p = p + jnp.where(odd, up, 0.0)
        tot = tot + jnp.where(odd, up, dn)
        yield
    for lev in range(3, nlev):
        w = 1 << lev
        nb = c // w
        cut = lambda x: [x[i * w:(i + 1) * w] for i in range(nb)]
        cat = lambda xs: jnp.concatenate(xs, axis=0)
        pb, tb, sb = cut(p), cut(tot), cut(scores)
        f = jnp.exp2(cat([pb[i] if i % 2 else tb[i] - pb[i] for i in range(nb)])).astype(BF16)
        q_side = lambda i: (i % 2 == 1) == (d == 0)
        if w >= BF16_ROWS:
            x = cat([a if q_side(i) else b for i, (a, b) in enumerate(zip(cut(qb), cut(kb)))]) * f
            dots = _dot_nt(x, x)
        else:
            dots = _dot_nt(qb * f, kb * f)
        db = cut(dots)
        scores = cat([sb[i] + mask_ref[d, lev, i * w:(i + 1) * w, :] * db[i] if q_side(i) else sb[i]
                      for i in range(nb)])
        p = cat([pb[i] + tb[i - 1] if i % 2 else pb[i] for i in range(nb)])
        tot = cat([tb[i] + tb[i ^ 1] for i in range(nb)])
        yield
    eq, ek = (p, tot - p) if d == 0 else (tot - p, p)
    st = st_ref[...]
    o = _dot(scores.astype(BF16), vb)
    o = o + _dot_nt(qb * jnp.exp2(eq).astype(BF16), st.astype(BF16))
    st_ref[...] = st * jnp.exp2(tot[0:1, :]) + _dot_tn(vb, kb * jnp.exp2(ek).astype(BF16))
    emit(o)


def _hg_rec_kernel(qf_ref, lff_ref, kf_ref, vf_ref, qb_ref, lfb_ref, kb_ref, vb_ref, mask_ref,
                   of_ref, ob_ref, stf_ref, stb_ref, *, nsub):
    @pl.when(pl.program_id(2) == 0)
    def _():
        stf_ref[...] = jnp.zeros_like(stf_ref)
        stb_ref[...] = jnp.zeros_like(stb_ref)

    c = HG_CHUNK

    def chain(d, rows):
        q_ref, lf_ref, k_ref, v_ref, o_ref, st_ref = ((qf_ref, lff_ref, kf_ref, vf_ref, of_ref, stf_ref) if d == 0 else
                                                      (qb_ref, lfb_ref, kb_ref, vb_ref, ob_ref, stb_ref))

        def emit(o):
            o_ref[rows, :] = o
        return _hg_chunk(q_ref[rows, :], lf_ref[rows, :], k_ref[rows, :], v_ref[rows, :], mask_ref, d, st_ref, emit)

    for j in range(nsub):
        _round_robin([chain(0, slice(j * c, (j + 1) * c)), chain(1, slice((nsub - 1 - j) * c, (nsub - j) * c))])


def _hg_post_kernel(of_ref, ob_ref, g_ref, h_ref, ng_ref, wo_ref, lg_ref, lb_ref, o_ref):
    o = of_ref[...] + ob_ref[...]
    hd = HG_HEAD_DIM
    parts = []
    for i in range(o.shape[1] // hd):
        oh = o[:, i * hd:(i + 1) * hd]
        parts.append(oh * lax.rsqrt(jnp.mean(oh * oh, axis=-1, keepdims=True) + RMS_EPS))
    on = jnp.concatenate(parts, axis=1) * ng_ref[...]
    gate = g_ref[...]
    y = on * (gate * jax.nn.sigmoid(gate))
    m = _dot(y.astype(BF16), wo_ref[...])
    o_ref[...] = _layer_norm(DEEPNORM_ALPHA * h_ref[...] + m, lg_ref[...], lb_ref[...])


def _hg_mixer_ln(h, bsz, seq, w_in, lb, norm_g, w_out, ln_g, ln_b, tm=512):
    t, d = h.shape
    hd = HG_HEAD_DIM
    nh = d // hd
    blk = min(HG_BLOCK, seq)
    nblk = seq // blk
    lb = lb.astype(F32).reshape(1, d)
    la, l1m, onem = jnp.log2(lb), jnp.log1p(-lb) * math.log2(math.e), 1.0 - lb
    row = lambda v: v.reshape(1, d)
    tile = pl.BlockSpec((tm, d), lambda i: (i, 0))
    act = lambda dt: jax.ShapeDtypeStruct((t, d), dt)
    qa, lff, lfb, kf, kb, v, gate = pl.pallas_call(
        _hg_in_kernel,
        grid=(t // tm,),
        in_specs=[tile, pl.BlockSpec(w_in.shape, lambda i: (0, 0), pipeline_mode=pl.Buffered(1)),
                  _const_spec((1, d)), _const_spec((1, d)), _const_spec((1, d))],
        out_specs=[tile] * 7,
        out_shape=[act(BF16), act(F32), act(F32), act(BF16), act(BF16), act(BF16), act(F32)],
        scratch_shapes=[pltpu.VMEM((HG_IN_AHEAD + 1, HG_IN_PIECE[1], HG_IN_PIECE[0]), F32)],
        compiler_params=_params("parallel"),
        name="hg_in",
    )(h, w_in.astype(BF16), la, l1m, onem)
    masks = _hg_masks(HG_CHUNK)
    fw = pl.BlockSpec((blk, hd), lambda b, hh, i: (b * nblk + i, hh))
    bw = pl.BlockSpec((blk, hd), lambda b, hh, i: (b * nblk + nblk - 1 - i, hh))
    o_fw, o_bw = pl.pallas_call(
        functools.partial(_hg_rec_kernel, nsub=blk // HG_CHUNK),
        grid=(bsz, nh, nblk),
        in_specs=[fw, fw, fw, fw, bw, bw, bw, bw, _const_spec(masks.shape)],
        out_specs=[fw, bw],
        out_shape=[act(F32)] * 2,
        scratch_shapes=[pltpu.VMEM((hd, hd), F32), pltpu.VMEM((hd, hd), F32)],
        compiler_params=_params("parallel", "parallel", "arbitrary"),
        name="hg_rec",
    )(qa, lff, kf, v, qa, lfb, kb, v, masks)

    return pl.pallas_call(
        _hg_post_kernel,
        grid=(t // tm,),
        in_specs=[tile, tile, tile, tile, _const_spec((1, d)),
                  _const_spec((d, d)), _const_spec((1, d)), _const_spec((1, d))],
        out_specs=tile,
        out_shape=act(F32),
        compiler_params=_params("parallel"),
        name="hg_post",
    )(o_fw, o_bw, gate, h, jnp.tile(norm_g.astype(F32), nh).reshape(1, d), w_out.astype(BF16), row(ln_g), row(ln_b))


def _mla_proj_kernel(x_ref, pos_ref, winT_ref, gq_ref, gkv_ref, wqT_ref, wkvT_ref, invf_ref,
                     qT_ref, k_ref, vT_ref):
    ql, kvl, half = MLA_Q_LORA, MLA_KV_LORA, MLA_ROPE // 2
    dq = MLA_NOPE + MLA_ROPE
    lat = _dot_nt(winT_ref[...], x_ref[...].astype(BF16))
    q_lat, kv_lat, k_rope = lat[:ql], lat[ql:ql + kvl], lat[ql + kvl:]

    def rms(v, g):
        return (v * lax.rsqrt(jnp.mean(v * v, axis=0, keepdims=True) + RMS_EPS) * g).astype(BF16)

    q = _dot(wqT_ref[...], rms(q_lat, gq_ref[...]))
    kv = _dot(wkvT_ref[...], rms(kv_lat, gkv_ref[...]))
    ang = invf_ref[...] * pos_ref[0].astype(F32)
    cos, sin = jnp.cos(ang), jnp.sin(ang)

    def rope(t1, t2):
        return t1 * cos - t2 * sin, t1 * sin + t2 * cos

    scale = dq ** -0.5 * math.log2(math.e)
    ones_rows = (lax.broadcasted_iota(jnp.int32, (BF16_ROWS, lat.shape[1]), 0) == 0).astype(BF16)
    k1, k2 = rope(k_rope[:half], k_rope[half:])
    pad = jnp.zeros((MLA_QK_PAD - dq, lat.shape[1]), F32)
    k_tail = jnp.concatenate([k1, k2, pad], axis=0).T.astype(BF16)
    for hh in range(MLA_HEADS):
        qh = q[hh * dq:(hh + 1) * dq]
        q1, q2 = rope(qh[MLA_NOPE:MLA_NOPE + half], qh[MLA_NOPE + half:])
        qT_ref[0, hh, 0:MLA_NOPE, :] = (qh[:MLA_NOPE] * scale).astype(BF16)
        qT_ref[0, hh, MLA_NOPE:MLA_NOPE + half, :] = (q1 * scale).astype(BF16)
        qT_ref[0, hh, MLA_NOPE + half:dq, :] = (q2 * scale).astype(BF16)
        qT_ref[0, hh, dq:, :] = pad.astype(BF16)
        kvh = kv[hh * (MLA_NOPE + MLA_V):(hh + 1) * (MLA_NOPE + MLA_V)]
        k_ref[0, hh, 0, :, 0:MLA_NOPE] = kvh[:MLA_NOPE].T.astype(BF16)
        k_ref[0, hh, 0, :, MLA_NOPE:] = k_tail
        vT_ref[0, hh, 0, 0:MLA_V, :] = kvh[MLA_NOPE:].astype(BF16)
        vT_ref[0, hh, 0, MLA_V:, :] = ones_rows


def _attn_kernel(qT_ref, k_ref, vT_ref, o_ref, s0_sc, s1_sc, t0_sc, t1_sc, m_sc, acc_sc, *, nkv, tqs, tks):
    dv = MLA_V
    s_sc, t_sc = (s0_sc, s1_sc), (t0_sc, t1_sc)
    tk, tq = s0_sc.shape
    m_sc[...] = jnp.full_like(m_sc, -jnp.inf)
    acc_sc[...] = jnp.zeros_like(acc_sc)

    def scores(j, par, c):
        cols = slice(c * tqs, (c + 1) * tqs)
        s = _dot(k_ref[0, 0, j], qT_ref[0, 0, :, cols])
        s_sc[par][:, cols] = s
        t_sc[par][:, cols] = jnp.max(s, axis=0, keepdims=True)

    def softmax_pv(j, par, c):
        cols = slice(c * tqs, (c + 1) * tqs)
        m_prev = m_sc[:, cols]
        m_new = jnp.maximum(m_prev, t_sc[par][:, cols])
        acc = jnp.exp2(m_prev - m_new) * acc_sc[:, cols]
        for r in range(tk // tks):
            rows = slice(r * tks, (r + 1) * tks)
            p = jnp.exp2(s_sc[par][rows, cols] - m_new).astype(BF16)
            acc = acc + _dot(vT_ref[0, 0, j, :, rows], p)
        acc_sc[:, cols] = acc
        m_sc[:, cols] = m_new

    nsub = tq // tqs

    def step(j_scores, j_soft):
        for c in range(nsub):
            if j_scores is not None:
                scores(j_scores[0], j_scores[1], c)
            if j_soft is not None:
                softmax_pv(j_soft[0], j_soft[1], c)

    step((0, 0), None)

    def body(jj, carry):
        j = 2 * jj
        step((j + 1, 1), (j, 0))
        step((j + 2, 0), (j + 1, 1))
        return carry

    lax.fori_loop(0, nkv // 2 - 1, body, 0, unroll=3)
    step((nkv - 1, 1), (nkv - 2, 0))
    step(None, (nkv - 1, 1))
    acc = acc_sc[...]
    o_ref[0, 0] = (acc[:dv] / acc[dv:dv + 1]).astype(o_ref.dtype)


def _mla_out_kernel(oT_ref, h_ref, wo_ref, g_ref, b_ref, o_ref):
    oT = oT_ref[0]
    oT = oT.reshape(oT.shape[0] * oT.shape[1], oT.shape[2])
    m = _dot_tn(oT, wo_ref[...])
    o_ref[...] = _layer_norm(DEEPNORM_ALPHA * h_ref[...] + m, g_ref[...], b_ref[...])


def _mla_mixer_ln(h, positions, bsz, seq, w_in, q_norm_g, w_q_b, kv_norm_g, w_kv_b, w_out, ln_g, ln_b,
                  tm=512, tq=4096, tqs=256, tks=256):
    t, d = h.shape
    nh, dq, dv = MLA_HEADS, MLA_NOPE + MLA_ROPE, MLA_V
    half = MLA_ROPE // 2
    inv_freq = 1.0 / (ROPE_THETA ** (jnp.arange(half, dtype=F32) * (2.0 / MLA_ROPE)))
    nlat = w_in.shape[1]
    nt = seq // tm
    dp = MLA_QK_PAD
    qT, kc, vT = pl.pallas_call(
        _mla_proj_kernel,
        grid=(bsz, nt),
        in_specs=[pl.BlockSpec((tm, d), lambda b, i: (b * nt + i, 0)),
                  pl.BlockSpec((1, 1, tm), lambda b, i: (b, 0, i)),
                  _const_spec((nlat, d)), _const_spec((MLA_Q_LORA, 1)), _const_spec((MLA_KV_LORA, 1)),
                  _const_spec((nh * dq, MLA_Q_LORA)), _const_spec((nh * (MLA_NOPE + dv), MLA_KV_LORA)),
                  _const_spec((half, 1))],
        out_specs=[pl.BlockSpec((1, nh, dp, tm), lambda b, i: (b, 0, 0, i)),
                   pl.BlockSpec((1, nh, 1, tm, dp), lambda b, i: (b, 0, i, 0, 0)),
                   pl.BlockSpec((1, nh, 1, dv + BF16_ROWS, tm), lambda b, i: (b, 0, i, 0, 0))],
        out_shape=[jax.ShapeDtypeStruct((bsz, nh, dp, seq), BF16),
                   jax.ShapeDtypeStruct((bsz, nh, nt, tm, dp), BF16),
                   jax.ShapeDtypeStruct((bsz, nh, nt, dv + BF16_ROWS, tm), BF16)],
        compiler_params=_params("parallel", "parallel"),
        name="mla_proj",
    )(h, positions.reshape(bsz, 1, seq), w_in.T.astype(BF16), q_norm_g.astype(F32).reshape(-1, 1),
      kv_norm_g.astype(F32).reshape(-1, 1), w_q_b.T.astype(BF16), w_kv_b.T.astype(BF16),
      inv_freq.reshape(half, 1))

    tk, nkv = tm, nt
    tq = min(tq, seq)
    oT = pl.pallas_call(
        functools.partial(_attn_kernel, nkv=nkv, tqs=tqs, tks=tks),
        grid=(bsz, nh, seq // tq),
        in_specs=[pl.BlockSpec((1, 1, dp, tq), lambda b, hh, i: (b, hh, 0, i)),
                  pl.BlockSpec((1, 1, nkv, tk, dp), lambda b, hh, i: (b, hh, 0, 0, 0),
                               pipeline_mode=pl.Buffered(1)),
                  pl.BlockSpec((1, 1, nkv, dv + BF16_ROWS, tk), lambda b, hh, i: (b, hh, 0, 0, 0),
                               pipeline_mode=pl.Buffered(1))],
        out_specs=pl.BlockSpec((1, 1, dv, tq), lambda b, hh, i: (b, hh, 0, i)),
        out_shape=jax.ShapeDtypeStruct((bsz, nh, dv, seq), BF16),
        scratch_shapes=[pltpu.VMEM((tk, tq), F32), pltpu.VMEM((tk, tq), F32),
                        pltpu.VMEM((1, tq), F32), pltpu.VMEM((1, tq), F32),
                        pltpu.VMEM((1, tq), F32), pltpu.VMEM((dv + BF16_ROWS, tq), F32)],
        compiler_params=_params("parallel", "parallel", "arbitrary"),
        name="mla_attn",
    )(qT, kc, vT)

    return pl.pallas_call(
        _mla_out_kernel,
        grid=(bsz, nt),
        in_specs=[pl.BlockSpec((1, nh, dv, tm), lambda b, i: (b, 0, 0, i)),
                  pl.BlockSpec((tm, d), lambda b, i: (b * nt + i, 0)),
                  _const_spec((nh * dv, d)), _const_spec((1, d)), _const_spec((1, d))],
        out_specs=pl.BlockSpec((tm, d), lambda b, i: (b * nt + i, 0)),
        out_shape=jax.ShapeDtypeStruct((t, d), F32),
        compiler_params=_params("parallel", "parallel"),
        name="mla_out",
    )(oT, h, w_out.astype(BF16), ln_g.reshape(1, d), ln_b.reshape(1, d))


def kernel(x, positions, ln_mix_g, ln_mix_b, ln_ffn_g, ln_ffn_b, ffn_w_in, ffn_w_out, s5_w_in, s5_lam_re, s5_lam_im, s5_log_step, s5_b_re, s5_b_im, s5_c_re, s5_c_im, s5_d, s5_w_glu, s5_b_glu, s5_w_out, hg_w_in, hg_lower_bound, hg_norm_g, hg_w_out, mla_w_in, mla_q_norm_g, mla_w_q_b, mla_kv_norm_g, mla_w_kv_b, mla_w_out):
    bsz, seq, d = x.shape
    lbs = jax.nn.softmax(hg_lower_bound.astype(F32), axis=0)
    lbs = jnp.cumsum(lbs, axis=0) - lbs[0]
    h = x.reshape(bsz * seq, d)
    for layer in range(DEPTH):
        kind = layer % N_MIXERS
        slot = layer // N_MIXERS
        lg, lb = ln_mix_g[layer], ln_mix_b[layer]
        if kind == 0:
            h = _s5_mixer_ln(h, bsz, seq, s5_w_in[slot], s5_lam_re[slot], s5_lam_im[slot], s5_log_step[slot],
                             s5_b_re[slot], s5_b_im[slot], s5_c_re[slot], s5_c_im[slot], s5_d[slot],
                             s5_w_glu[slot], s5_b_glu[slot], s5_w_out[slot], lg, lb)
        elif kind == 1:
            h = _hg_mixer_ln(h, bsz, seq, hg_w_in[slot], lbs[layer], hg_norm_g[slot], hg_w_out[slot], lg, lb)
        else:
            h = _mla_mixer_ln(h, positions, bsz, seq, mla_w_in[slot], mla_q_norm_g[slot], mla_w_q_b[slot],
                              mla_kv_norm_g[slot], mla_w_kv_b[slot], mla_w_out[slot], lg, lb)
        h = _ffn_ln(h, ffn_w_in[layer], ffn_w_out[layer], ln_ffn_g[layer], ln_ffn_b[layer])
    return h.reshape(bsz, seq, d)
```

```python
import functools
import math

import jax
import jax.numpy as jnp
from jax import lax
from jax.experimental import pallas as pl
from jax.experimental.pallas import tpu as pltpu

F32 = jnp.float32
BF16 = jnp.bfloat16
LANES = 128
BF16_ROWS = 16

DEPTH = 4
N_MIXERS = 3
S5_GROUP = 16
S5_STATE = 64
S5_CHUNK = 64
S5_POST_SUB = 256
S5_PW_ROWS = 72
HG_HEAD_DIM = 128
HG_CHUNK = 128
HG_BLOCK = 2048
HG_IN_PIECE = (256, 256)
HG_IN_AHEAD = 2
MLA_HEADS = 8
MLA_NOPE = 128
MLA_ROPE = 64
MLA_V = 128
MLA_QK_PAD = 256
MLA_Q_LORA = 384
MLA_KV_LORA = 256
ROPE_THETA = 10000.0
DEEPNORM_ALPHA = (2 * DEPTH) ** 0.25
LN_EPS = 1e-5
RMS_EPS = 1e-6
FFN_CHUNK = 1024
VMEM_LIMIT = 56 * 1024 * 1024


def _params(*sem):
    return pltpu.CompilerParams(dimension_semantics=sem, vmem_limit_bytes=VMEM_LIMIT)


def _dot(a, b):
    return jnp.dot(a, b, preferred_element_type=F32)


def _dot_nt(a, b):
    return lax.dot_general(a, b, (((1,), (1,)), ((), ())), preferred_element_type=F32)


def _dot_tn(a, b):
    return lax.dot_general(a, b, (((0,), (0,)), ((), ())), preferred_element_type=F32)


def _layer_norm(y, g, b):
    mu = jnp.mean(y, axis=-1, keepdims=True)
    d = y - mu
    var = jnp.mean(d * d, axis=-1, keepdims=True)
    return d * lax.rsqrt(var + LN_EPS) * g + b


def _const_spec(shape):
    return pl.BlockSpec(shape, lambda *_: (0,) * len(shape))


def _round_robin(gens, stagger=0):
    live = list(enumerate(gens))
    rnd = 0
    while live:
        live = [(i, g) for i, g in live if rnd < i * stagger or next(g, StopIteration) is not StopIteration]
        rnd += 1


def _ffn_kernel(x_ref, win_ref, wout_ref, g_ref, b_ref, o_ref, *, hid):
    x = x_ref[...]
    xb = x.astype(BF16)
    acc = DEEPNORM_ALPHA * x
    for lo in range(0, hid, FFN_CHUNK):
        w = min(FFN_CHUNK, hid - lo)
        gate = _dot(xb, win_ref[:, lo:lo + w])
        up = _dot(xb, win_ref[:, hid + lo:hid + lo + w])
        mid = (gate * jax.nn.sigmoid(gate) * up).astype(BF16)
        acc = acc + _dot(mid, wout_ref[lo:lo + w, :])
    o_ref[...] = _layer_norm(acc, g_ref[...], b_ref[...])


def _ffn_ln(x, w_in, w_out, g, b, tm=1024):
    t, d = x.shape
    hid = w_out.shape[0]
    return pl.pallas_call(
        functools.partial(_ffn_kernel, hid=hid),
        grid=(t // tm,),
        in_specs=[pl.BlockSpec((tm, d), lambda i: (i, 0)),
                  pl.BlockSpec((d, 2 * hid), lambda i: (0, 0), pipeline_mode=pl.Buffered(1)),
                  pl.BlockSpec((hid, d), lambda i: (0, 0), pipeline_mode=pl.Buffered(1)),
                  _const_spec((1, d)), _const_spec((1, d))],
        out_specs=pl.BlockSpec((tm, d), lambda i: (i, 0)),
        out_shape=jax.ShapeDtypeStruct((t, d), F32),
        compiler_params=_params("parallel"),
        name="ffn_ln",
    )(x, w_in.astype(BF16), w_out.astype(BF16), g.reshape(1, d), b.reshape(1, d))


def _s5_tables(lam_re, lam_im, log_step, b_re, b_im, c_re, c_im, nlev):
    L = S5_CHUNK
    lam = lax.complex(lam_re.astype(F32), lam_im.astype(F32))
    step = jnp.exp(log_step.astype(F32))[..., None]
    lam_dt = lam * step
    lam_bar = jnp.exp(lam_dt)
    b_bar = ((lam_bar - 1.0) / lam)[..., None] * lax.complex(b_re.astype(F32), b_im.astype(F32))
    g, p = lam.shape[1], lam.shape[2]
    taus = jnp.arange(S5_PW_ROWS, dtype=F32)
    pw = jnp.exp(lam_dt[:, None] * taus[None, :, None, None])
    pr, pi = jnp.real(pw), jnp.imag(pw)
    cat = lambda a, b: jnp.concatenate([a, b], axis=-1)
    pw_tab = jnp.stack([cat(pr, pr), cat(pi, pi)], axis=1)
    pw_tab = pw_tab.transpose(3, 0, 1, 2, 4).reshape(g, 4, S5_PW_ROWS, 2 * p)
    bt = jnp.swapaxes(b_bar, 2, 3)
    bre, bim = jnp.real(bt), jnp.imag(bt)
    cre, cim = c_re.astype(F32), c_im.astype(F32)
    c2 = cat(cre, -cim)
    bc_tab = jnp.stack([cat(bre, bim), cat(-bim, bre), c2, cat(-cim, -cre)], axis=2)
    bc_tab = bc_tab.transpose(1, 0, 2, 3, 4).reshape(g, 8, S5_GROUP, 2 * p)
    cmat = cat(c2[0], c2[1]).astype(BF16)
    k2 = (L * 2.0 ** jnp.arange(nlev, dtype=F32))
    a = jnp.exp(lam_dt[:, None] * k2[None, :, None, None])
    ar, ai = jnp.real(a), jnp.imag(a)
    p1 = cat(ar, ar).transpose(2, 0, 3, 1).reshape(g, 4 * p, nlev)
    p2 = cat(-ai, ai).transpose(2, 0, 3, 1).reshape(g, 4 * p, nlev)
    return pw_tab, bc_tab, cmat, p1, p2


def _s5_core_kernel(u_ref, pw_ref, bc_ref, cm_ref, p1_ref, p2_ref, y_ref, ut_sc, e_sc, gt_sc, bs_sc, tt_sc,
                    *, nc, nlev):
    L, hh, p, two_p = S5_CHUNK, S5_GROUP, S5_STATE, 2 * S5_STATE
    lh = L * hh
    n = ut_sc.shape[1]

    def blk(d, kind, tau):
        base = 4 * d + 2 * kind
        return (bc_ref[0, base] * pw_ref[0, 2 * d, tau:tau + 1, :]
                + bc_ref[0, base + 1] * pw_ref[0, 2 * d + 1, tau:tau + 1, :])

    zero = jnp.zeros((hh, two_p), BF16)
    for i in range(L):
        rows = slice(i * hh, (i + 1) * hh)
        ut_sc[rows, :] = u_ref[:, i * n:(i + 1) * n]
        e_sc[rows, :two_p] = blk(0, 0, L - 1 - i).astype(BF16)
        e_sc[rows, two_p:] = blk(1, 0, i).astype(BF16)
        gt_sc[rows, :two_p] = blk(0, 1, i + 1).astype(BF16)
        gt_sc[rows, two_p:] = blk(1, 1, L - i).astype(BF16)
    for rho in range(2 * L):
        rows = slice(rho * hh, (rho + 1) * hh)
        lag = L - 1 - rho
        bs_sc[rows, :two_p] = blk(0, 0, lag).astype(BF16) if 0 <= lag < L else zero
        bs_sc[rows, two_p:] = blk(1, 0, -lag).astype(BF16) if 0 <= -lag < L else zero
    strip = _dot_nt(cm_ref[0], bs_sc[...])
    for r in range(LANES // hh):
        rot = strip if r == 0 else pltpu.roll(strip, 2 * lh - r * hh, 1)
        for j in range(L):
            a, rr = divmod(L - 1 - j, LANES // hh)
            if rr == r:
                tt_sc[j * hh:(j + 1) * hh, :] = rot[:, LANES * a:LANES * a + lh].astype(BF16)

    ut = ut_sc[...]
    y = _dot(tt_sc[...], ut)
    s = _dot_tn(e_sc[...], ut)
    cidx = lax.broadcasted_iota(jnp.int32, (two_p, n), 1) & (nc - 1)

    def cmul(x, k, d):
        a1 = p1_ref[0, d * two_p:(d + 1) * two_p, k:k + 1]
        a2 = p2_ref[0, d * two_p:(d + 1) * two_p, k:k + 1]
        return x * a1 + jnp.concatenate([x[p:], x[:p]], axis=0) * a2

    xf, xb = s[:two_p], s[two_p:]
    for k in range(nlev):
        sh = 1 << k
        xf = xf + jnp.where(cidx >= sh, cmul(pltpu.roll(xf, sh, 1), k, 0), 0.0)
        xb = xb + jnp.where(cidx < nc - sh, cmul(pltpu.roll(xb, n - sh, 1), k, 1), 0.0)
    hf = jnp.where(cidx >= 1, pltpu.roll(xf, 1, 1), 0.0)
    hb = jnp.where(cidx < nc - 1, pltpu.roll(xb, n - 1, 1), 0.0)
    y = y + _dot(gt_sc[...], jnp.concatenate([hf, hb], axis=0).astype(BF16))
    for j in range(L):
        y_ref[:, j * n:(j + 1) * n] = y[j * hh:(j + 1) * hh, :]


def _s5_core(ut, tables, n, nc, nlev):
    d, t = ut.shape
    hh, two_p = S5_GROUP, 2 * S5_STATE
    lh = S5_CHUNK * hh
    blk = lambda a: pl.BlockSpec((1,) + a.shape[1:], lambda i: (i,) + (0,) * (a.ndim - 1))
    return pl.pallas_call(
        functools.partial(_s5_core_kernel, nc=nc, nlev=nlev),
        grid=(d // hh,),
        in_specs=[pl.BlockSpec((hh, t), lambda i: (i, 0))] + [blk(a) for a in tables],
        out_specs=pl.BlockSpec((hh, t), lambda i: (i, 0)),
        out_shape=jax.ShapeDtypeStruct((d, t), F32),
        scratch_shapes=[pltpu.VMEM((lh, n), BF16),
                        pltpu.VMEM((lh, 2 * two_p), BF16), pltpu.VMEM((lh, 2 * two_p), BF16),
                        pltpu.VMEM((2 * lh, 2 * two_p), BF16), pltpu.VMEM((lh, lh), BF16)],
        compiler_params=_params("parallel"),
        name="s5_core",
    )(ut, *tables)


def _s5_in_kernel(x_ref, wT_ref, o_ref):
    o_ref[...] = _dot_nt(wT_ref[...], x_ref[...].astype(BF16)).astype(o_ref.dtype)


def _s5_post_kernel(yT_ref, x_ref, winT_ref, d_ref, wgT_ref, bg_ref, wo_ref, g_ref, b_ref, o_ref):
    ts = S5_POST_SUB

    def sub_tile(i):
        tok = slice(i * ts, (i + 1) * ts)
        x = x_ref[tok, :]
        u = _dot_nt(winT_ref[...], x.astype(BF16))
        yield
        y = jax.nn.gelu(yT_ref[:, tok] + d_ref[...] * u)
        yield
        z = _dot(wgT_ref[...], y.astype(BF16)) + bg_ref[...]
        yield
        y = y * jax.nn.sigmoid(z)
        yield
        m = _dot_tn(y.astype(BF16), wo_ref[...])
        yield
        o_ref[tok, :] = _layer_norm(DEEPNORM_ALPHA * x + m, g_ref[...], b_ref[...])

    _round_robin([sub_tile(i) for i in range(x_ref.shape[0] // ts)], stagger=1)


def _s5_mixer_ln(h, bsz, seq, w_in, lam_re, lam_im, log_step, b_re, b_im, c_re, c_im,
                 d_skip, w_glu, b_glu, w_out, ln_g, ln_b, tm=1024):
    t, d = h.shape
    L = S5_CHUNK
    nc = seq // L
    n = bsz * nc
    nlev = max(1, (nc - 1).bit_length())
    assert nc & (nc - 1) == 0
    hp = h.reshape(n, L, d).transpose(1, 0, 2).reshape(t, d)
    w_in_t = w_in.T.astype(BF16)
    tile = pl.BlockSpec((tm, d), lambda i: (i, 0))
    tile_t = pl.BlockSpec((d, tm), lambda i: (0, i))
    ut = pl.pallas_call(
        _s5_in_kernel,
        grid=(t // tm,),
        in_specs=[tile, _const_spec((d, d))],
        out_specs=tile_t,
        out_shape=jax.ShapeDtypeStruct((d, t), BF16),
        compiler_params=_params("parallel"),
        name="s5_in",
    )(hp, w_in_t)
    tables = _s5_tables(lam_re, lam_im, log_step, b_re, b_im, c_re, c_im, nlev)
    yt = _s5_core(ut, tables, n, nc, nlev)
    row = lambda v: v.reshape(1, d)
    col = lambda v: v.astype(F32).reshape(d, 1)
    out = pl.pallas_call(
        _s5_post_kernel,
        grid=(t // tm,),
        in_specs=[tile_t, tile, _const_spec((d, d)), _const_spec((d, 1)), _const_spec((d, d)), _const_spec((d, 1)),
                  _const_spec((d, d)), _const_spec((1, d)), _const_spec((1, d))],
        out_specs=tile,
        out_shape=jax.ShapeDtypeStruct((t, d), F32),
        compiler_params=_params("parallel"),
        name="s5_post",
    )(yt, hp, w_in_t, col(d_skip), w_glu.T.astype(BF16), col(b_glu), w_out.astype(BF16), row(ln_g), row(ln_b))
    return out.reshape(L, n, d).transpose(1, 0, 2).reshape(t, d)


def _hg_masks(c):
    nlev = c.bit_length() - 1
    t = jnp.arange(c)[:, None]
    s = jnp.arange(c)[None, :]
    ms = []
    for lev in range(nlev):
        w = 1 << lev
        ms.append(((t & w) != 0) & ((s & w) == 0) & ((t >> (lev + 1)) == (s >> (lev + 1))))
    ms.append(t == s)
    fw = jnp.stack(ms).astype(F32)
    return jnp.stack([fw, fw.transpose(0, 2, 1)])


def _hg_gate(z, la2, l1m2, onem):
    zs = z * math.log2(math.e)
    ls2 = jnp.minimum(zs, 0.0) - jnp.log2(1.0 + jnp.exp2(-jnp.abs(zs)))
    b2 = l1m2 + ls2
    lf2 = jnp.maximum(la2, b2) + jnp.log2(1.0 + jnp.exp2(-jnp.abs(la2 - b2)))
    return lf2, onem * jnp.exp2(ls2 - zs)


def _hg_in_kernel(x_ref, w_ref, la_ref, l1m_ref, onem_ref, q_ref, lff_ref, lfb_ref, kf_ref, kb_ref, v_ref, g_ref,
                  y_sc):
    d = q_ref.shape[1]
    pw, ph = HG_IN_PIECE
    npc = d // pw
    nrb = x_ref.shape[0] // ph
    xs = [x_ref[r * ph:(r + 1) * ph, :].astype(BF16) for r in range(nrb)]
    piece = lambda n, r: _dot(xs[r], w_ref[:, n * pw:(n + 1) * pw])

    def finish(n, r, y):
        kind, cols, rows = n // npc, slice((n % npc) * pw, (n % npc + 1) * pw), slice(r * ph, (r + 1) * ph)
        if kind == 0:
            q_ref[rows, cols] = (y * jax.nn.sigmoid(y)).astype(q_ref.dtype)
        elif kind in (1, 2):
            lf_ref, k_ref = (lff_ref, kf_ref) if kind == 1 else (lfb_ref, kb_ref)
            lf_ref[rows, cols], k = _hg_gate(y, la_ref[:, cols], l1m_ref[:, cols], onem_ref[:, cols])
            k_ref[rows, cols] = k.astype(k_ref.dtype)
        elif kind == 3:
            v_ref[rows, cols] = y.astype(v_ref.dtype)
        else:
            g_ref[rows, cols] = y

    heavy = [(n, r) for n in range(5 * npc) if n // npc in (1, 2) for r in range(nrb)]
    light = [(n, r) for n in range(5 * npc) if n // npc not in (1, 2) for r in range(nrb)]
    order = []
    while heavy or light:
        order += heavy[:1] + light[:1]
        heavy, light = heavy[1:], light[1:]
    nslot = y_sc.shape[0]
    for i in range(len(order) + HG_IN_AHEAD):
        if i < len(order):
            y_sc[i % nslot] = piece(*order[i])
        if i >= HG_IN_AHEAD:
            finish(*order[i - HG_IN_AHEAD], y_sc[(i - HG_IN_AHEAD) % nslot])


def _hg_chunk(qb, lf, kb, vb, mask_ref, d, st_ref, emit):
    c = qb.shape[0]
    nlev = c.bit_length() - 1
    row = lax.broadcasted_iota(jnp.int32, lf.shape, 0)
    scores = mask_ref[d, nlev] * _dot_nt(qb, kb)
    p, tot = (lf if d == 0 else jnp.zeros_like(lf)), lf
    yield
    for lev in range(min(nlev, 3)):
        w = 1 << lev
        odd = (row & w) != 0
        f = jnp.exp2(jnp.where(odd, p, tot - p)).astype(BF16)
        scores = scores + mask_ref[d, lev] * _dot_nt(qb * f, kb * f)
        up = pltpu.roll(tot, w, 0)
        dn = pltpu.roll(tot, c - w, 0)
        p = p + jnp.where(odd, up, 0.0)
        tot = tot + jnp.where(odd, up, dn)
        yield
    for lev in range(3, nlev):
        w = 1 << lev
        nb = c // w
        cut = lambda x: [x[i * w:(i + 1) * w] for i in range(nb)]
        cat = lambda xs: jnp.concatenate(xs, axis=0)
        pb, tb, sb = cut(p), cut(tot), cut(scores)
        f = jnp.exp2(cat([pb[i] if i % 2 else tb[i] - pb[i] for i in range(nb)])).astype(BF16)
        q_side = lambda i: (i % 2 == 1) == (d == 0)
        if w >= BF16_ROWS:
            x = cat([a if q_side(i) else b for i, (a, b) in enumerate(zip(cut(qb), cut(kb)))]) * f
            dots = _dot_nt(x, x)
        else:
            dots = _dot_nt(qb * f, kb * f)
        db = cut(dots)
        scores = cat([sb[i] + mask_ref[d, lev, i * w:(i + 1) * w, :] * db[i] if q_side(i) else sb[i]
                      for i in range(nb)])
        p = cat([pb[i] + tb[i - 1] if i % 2 else pb[i] for i in range(nb)])
        tot = cat([tb[i] + tb[i ^ 1] for i in range(nb)])
        yield
    eq, ek = (p, tot - p) if d == 0 else (tot - p, p)
    st = st_ref[...]
    o = _dot(scores.astype(BF16), vb)
    o = o + _dot_nt(qb * jnp.exp2(eq).astype(BF16), st.astype(BF16))
    st_ref[...] = st * jnp.exp2(tot[0:1, :]) + _dot_tn(vb, kb * jnp.exp2(ek).astype(BF16))
    emit(o)


def _hg_rec_kernel(qf_ref, lff_ref, kf_ref, vf_ref, qb_ref, lfb_ref, kb_ref, vb_ref, mask_ref,
                   of_ref, ob_ref, stf_ref, stb_ref, *, nsub):
    @pl.when(pl.program_id(2) == 0)
    def _():
        stf_ref[...] = jnp.zeros_like(stf_ref)
        stb_ref[...] = jnp.zeros_like(stb_ref)

    c = HG_CHUNK

    def chain(d, rows):
        q_ref, lf_ref, k_ref, v_ref, o_ref, st_ref = ((qf_ref, lff_ref, kf_ref, vf_ref, of_ref, stf_ref) if d == 0 else
                                                      (qb_ref, lfb_ref, kb_ref, vb_ref, ob_ref, stb_ref))

        def emit(o):
            o_ref[rows, :] = o
        return _hg_chunk(q_ref[rows, :], lf_ref[rows, :], k_ref[rows, :], v_ref[rows, :], mask_ref, d, st_ref, emit)

    for j in range(nsub):
        _round_robin([chain(0, slice(j * c, (j + 1) * c)), chain(1, slice((nsub - 1 - j) * c, (nsub - j) * c))])


def _hg_post_kernel(of_ref, ob_ref, g_ref, h_ref, ng_ref, wo_ref, lg_ref, lb_ref, o_ref):
    o = of_ref[...] + ob_ref[...]
    hd = HG_HEAD_DIM
    parts = []
    for i in range(o.shape[1] // hd):
        oh = o[:, i * hd:(i + 1) * hd]
        parts.append(oh * lax.rsqrt(jnp.mean(oh * oh, axis=-1, keepdims=True) + RMS_EPS))
    on = jnp.concatenate(parts, axis=1) * ng_ref[...]
    gate = g_ref[...]
    y = on * (gate * jax.nn.sigmoid(gate))
    m = _dot(y.astype(BF16), wo_ref[...])
    o_ref[...] = _layer_norm(DEEPNORM_ALPHA * h_ref[...] + m, lg_ref[...], lb_ref[...])


def _hg_mixer_ln(h, bsz, seq, w_in, lb, norm_g, w_out, ln_g, ln_b, tm=512, tm_post=1024):
    t, d = h.shape
    hd = HG_HEAD_DIM
    nh = d // hd
    blk = min(HG_BLOCK, seq)
    nblk = seq // blk
    lb = lb.astype(F32).reshape(1, d)
    la, l1m, onem = jnp.log2(lb), jnp.log1p(-lb) * math.log2(math.e), 1.0 - lb
    row = lambda v: v.reshape(1, d)
    tile = pl.BlockSpec((tm, d), lambda i: (i, 0))
    act = lambda dt: jax.ShapeDtypeStruct((t, d), dt)
    qa, lff, lfb, kf, kb, v, gate = pl.pallas_call(
        _hg_in_kernel,
        grid=(t // tm,),
        in_specs=[tile, pl.BlockSpec(w_in.shape, lambda i: (0, 0), pipeline_mode=pl.Buffered(1)),
                  _const_spec((1, d)), _const_spec((1, d)), _const_spec((1, d))],
        out_specs=[tile] * 7,
        out_shape=[act(BF16), act(F32), act(F32), act(BF16), act(BF16), act(BF16), act(F32)],
        scratch_shapes=[pltpu.VMEM((HG_IN_AHEAD + 1, HG_IN_PIECE[1], HG_IN_PIECE[0]), F32)],
        compiler_params=_params("parallel"),
        name="hg_in",
    )(h, w_in.astype(BF16), la, l1m, onem)
    masks = _hg_masks(HG_CHUNK)
    fw = pl.BlockSpec((blk, hd), lambda b, hh, i: (b * nblk + i, hh))
    bw = pl.BlockSpec((blk, hd), lambda b, hh, i: (b * nblk + nblk - 1 - i, hh))
    o_fw, o_bw = pl.pallas_call(
        functools.partial(_hg_rec_kernel, nsub=blk // HG_CHUNK),
        grid=(bsz, nh, nblk),
        in_specs=[fw, fw, fw, fw, bw, bw, bw, bw, _const_spec(masks.shape)],
        out_specs=[fw, bw],
        out_shape=[act(F32)] * 2,
        scratch_shapes=[pltpu.VMEM((hd, hd), F32), pltpu.VMEM((hd, hd), F32)],
        compiler_params=_params("parallel", "parallel", "arbitrary"),
        name="hg_rec",
    )(qa, lff, kf, v, qa, lfb, kb, v, masks)

    tile = pl.BlockSpec((tm_post, d), lambda i: (i, 0))
    return pl.pallas_call(
        _hg_post_kernel,
        grid=(t // tm_post,),
        in_specs=[tile, tile, tile, tile, _const_spec((1, d)),
                  _const_spec((d, d)), _const_spec((1, d)), _const_spec((1, d))],
        out_specs=tile,
        out_shape=act(F32),
        compiler_params=_params("parallel"),
        name="hg_post",
    )(o_fw, o_bw, gate, h, jnp.tile(norm_g.astype(F32), nh).reshape(1, d), w_out.astype(BF16), row(ln_g), row(ln_b))


def _mla_proj_kernel(x_ref, pos_ref, winT_ref, gq_ref, gkv_ref, wqT_ref, wkvT_ref, invf_ref,
                     qT_ref, k_ref, vT_ref):
    ql, kvl, half = MLA_Q_LORA, MLA_KV_LORA, MLA_ROPE // 2
    dq = MLA_NOPE + MLA_ROPE
    lat = _dot_nt(winT_ref[...], x_ref[...].astype(BF16))
    q_lat, kv_lat, k_rope = lat[:ql], lat[ql:ql + kvl], lat[ql + kvl:]

    def rms(v, g):
        return (v * lax.rsqrt(jnp.mean(v * v, axis=0, keepdims=True) + RMS_EPS) * g).astype(BF16)

    q = _dot(wqT_ref[...], rms(q_lat, gq_ref[...]))
    kv = _dot(wkvT_ref[...], rms(kv_lat, gkv_ref[...]))
    ang = invf_ref[...] * pos_ref[0].astype(F32)
    cos, sin = jnp.cos(ang), jnp.sin(ang)

    def rope(t1, t2):
        return t1 * cos - t2 * sin, t1 * sin + t2 * cos

    scale = dq ** -0.5 * math.log2(math.e)
    ones_rows = (lax.broadcasted_iota(jnp.int32, (BF16_ROWS, lat.shape[1]), 0) == 0).astype(BF16)
    k1, k2 = rope(k_rope[:half], k_rope[half:])
    pad = jnp.zeros((MLA_QK_PAD - dq, lat.shape[1]), F32)
    k_tail = jnp.concatenate([k1, k2, pad], axis=0).T.astype(BF16)
    for hh in range(MLA_HEADS):
        qh = q[hh * dq:(hh + 1) * dq]
        q1, q2 = rope(qh[MLA_NOPE:MLA_NOPE + half], qh[MLA_NOPE + half:])
        qT_ref[0, hh, 0:MLA_NOPE, :] = (qh[:MLA_NOPE] * scale).astype(BF16)
        qT_ref[0, hh, MLA_NOPE:MLA_NOPE + half, :] = (q1 * scale).astype(BF16)
        qT_ref[0, hh, MLA_NOPE + half:dq, :] = (q2 * scale).astype(BF16)
        qT_ref[0, hh, dq:, :] = pad.astype(BF16)
        kvh = kv[hh * (MLA_NOPE + MLA_V):(hh + 1) * (MLA_NOPE + MLA_V)]
        k_ref[0, hh, 0, :, 0:MLA_NOPE] = kvh[:MLA_NOPE].T.astype(BF16)
        k_ref[0, hh, 0, :, MLA_NOPE:] = k_tail
        vT_ref[0, hh, 0, 0:MLA_V, :] = kvh[MLA_NOPE:].astype(BF16)
        vT_ref[0, hh, 0, MLA_V:, :] = ones_rows


def _attn_kernel(qT_ref, k_ref, vT_ref, o_ref, s0_sc, s1_sc, t0_sc, t1_sc, m_sc, acc_sc, *, nkv, tqs, tks):
    dv = MLA_V
    s_sc, t_sc = (s0_sc, s1_sc), (t0_sc, t1_sc)
    tk, tq = s0_sc.shape
    m_sc[...] = jnp.full_like(m_sc, -jnp.inf)
    acc_sc[...] = jnp.zeros_like(acc_sc)

    def scores(j, par, c):
        cols = slice(c * tqs, (c + 1) * tqs)
        s = _dot(k_ref[0, 0, j], qT_ref[0, 0, :, cols])
        s_sc[par][:, cols] = s
        t_sc[par][:, cols] = jnp.max(s, axis=0, keepdims=True)

    def softmax_pv(j, par, c):
        cols = slice(c * tqs, (c + 1) * tqs)
        m_prev = m_sc[:, cols]
        m_new = jnp.maximum(m_prev, t_sc[par][:, cols])
        acc = jnp.exp2(m_prev - m_new) * acc_sc[:, cols]
        for r in range(tk // tks):
            rows = slice(r * tks, (r + 1) * tks)
            p = jnp.exp2(s_sc[par][rows, cols] - m_new).astype(BF16)
            acc = acc + _dot(vT_ref[0, 0, j, :, rows], p)
        acc_sc[:, cols] = acc
        m_sc[:, cols] = m_new

    nsub = tq // tqs

    def step(j_scores, j_soft):
        for c in range(nsub):
            if j_scores is not None:
                scores(j_scores[0], j_scores[1], c)
            if j_soft is not None:
                softmax_pv(j_soft[0], j_soft[1], c)

    step((0, 0), None)

    def body(jj, carry):
        j = 2 * jj
        step((j + 1, 1), (j, 0))
        step((j + 2, 0), (j + 1, 1))
        return carry

    lax.fori_loop(0, nkv // 2 - 1, body, 0, unroll=3)
    step((nkv - 1, 1), (nkv - 2, 0))
    step(None, (nkv - 1, 1))
    acc = acc_sc[...]
    o_ref[0, 0] = (acc[:dv] / acc[dv:dv + 1]).astype(o_ref.dtype)


def _mla_out_kernel(oT_ref, h_ref, wo_ref, g_ref, b_ref, o_ref):
    oT = oT_ref[0]
    oT = oT.reshape(oT.shape[0] * oT.shape[1], oT.shape[2])
    m = _dot_tn(oT, wo_ref[...])
    o_ref[...] = _layer_norm(DEEPNORM_ALPHA * h_ref[...] + m, g_ref[...], b_ref[...])


def _mla_mixer_ln(h, positions, bsz, seq, w_in, q_norm_g, w_q_b, kv_norm_g, w_kv_b, w_out, ln_g, ln_b,
                  tm=512, tq=4096, tqs=256, tks=256, tm_out=1024):
    t, d = h.shape
    nh, dq, dv = MLA_HEADS, MLA_NOPE + MLA_ROPE, MLA_V
    half = MLA_ROPE // 2
    inv_freq = 1.0 / (ROPE_THETA ** (jnp.arange(half, dtype=F32) * (2.0 / MLA_ROPE)))
    nlat = w_in.shape[1]
    nt = seq // tm
    dp = MLA_QK_PAD
    qT, kc, vT = pl.pallas_call(
        _mla_proj_kernel,
        grid=(bsz, nt),
        in_specs=[pl.BlockSpec((tm, d), lambda b, i: (b * nt + i, 0)),
                  pl.BlockSpec((1, 1, tm), lambda b, i: (b, 0, i)),
                  _const_spec((nlat, d)), _const_spec((MLA_Q_LORA, 1)), _const_spec((MLA_KV_LORA, 1)),
                  _const_spec((nh * dq, MLA_Q_LORA)), _const_spec((nh * (MLA_NOPE + dv), MLA_KV_LORA)),
                  _const_spec((half, 1))],
        out_specs=[pl.BlockSpec((1, nh, dp, tm), lambda b, i: (b, 0, 0, i)),
                   pl.BlockSpec((1, nh, 1, tm, dp), lambda b, i: (b, 0, i, 0, 0)),
                   pl.BlockSpec((1, nh, 1, dv + BF16_ROWS, tm), lambda b, i: (b, 0, i, 0, 0))],
        out_shape=[jax.ShapeDtypeStruct((bsz, nh, dp, seq), BF16),
                   jax.ShapeDtypeStruct((bsz, nh, nt, tm, dp), BF16),
                   jax.ShapeDtypeStruct((bsz, nh, nt, dv + BF16_ROWS, tm), BF16)],
        compiler_params=_params("parallel", "parallel"),
        name="mla_proj",
    )(h, positions.reshape(bsz, 1, seq), w_in.T.astype(BF16), q_norm_g.astype(F32).reshape(-1, 1),
      kv_norm_g.astype(F32).reshape(-1, 1), w_q_b.T.astype(BF16), w_kv_b.T.astype(BF16),
      inv_freq.reshape(half, 1))

    tk, nkv = tm, nt
    tq = min(tq, seq)
    oT = pl.pallas_call(
        functools.partial(_attn_kernel, nkv=nkv, tqs=tqs, tks=tks),
        grid=(bsz, nh, seq // tq),
        in_specs=[pl.BlockSpec((1, 1, dp, tq), lambda b, hh, i: (b, hh, 0, i)),
                  pl.BlockSpec((1, 1, nkv, tk, dp), lambda b, hh, i: (b, hh, 0, 0, 0),
                               pipeline_mode=pl.Buffered(1)),
                  pl.BlockSpec((1, 1, nkv, dv + BF16_ROWS, tk), lambda b, hh, i: (b, hh, 0, 0, 0),
                               pipeline_mode=pl.Buffered(1))],
        out_specs=pl.BlockSpec((1, 1, dv, tq), lambda b, hh, i: (b, hh, 0, i)),
        out_shape=jax.ShapeDtypeStruct((bsz, nh, dv, seq), BF16),
        scratch_shapes=[pltpu.VMEM((tk, tq), F32), pltpu.VMEM((tk, tq), F32),
                        pltpu.VMEM((1, tq), F32), pltpu.VMEM((1, tq), F32),
                        pltpu.VMEM((1, tq), F32), pltpu.VMEM((dv + BF16_ROWS, tq), F32)],
        compiler_params=_params("parallel", "parallel", "arbitrary"),
        name="mla_attn",
    )(qT, kc, vT)

    to = min(tm_out, seq)
    no = seq // to
    return pl.pallas_call(
        _mla_out_kernel,
        grid=(bsz, no),
        in_specs=[pl.BlockSpec((1, nh, dv, to), lambda b, i: (b, 0, 0, i)),
                  pl.BlockSpec((to, d), lambda b, i: (b * no + i, 0)),
                  _const_spec((nh * dv, d)), _const_spec((1, d)), _const_spec((1, d))],
        out_specs=pl.BlockSpec((to, d), lambda b, i: (b * no + i, 0)),
        out_shape=jax.ShapeDtypeStruct((t, d), F32),
        compiler_params=_params("parallel", "parallel"),
        name="mla_out",
    )(oT, h, w_out.astype(BF16), ln_g.reshape(1, d), ln_b.reshape(1, d))


def kernel(x, positions, ln_mix_g, ln_mix_b, ln_ffn_g, ln_ffn_b, ffn_w_in, ffn_w_out, s5_w_in, s5_lam_re, s5_lam_im, s5_log_step, s5_b_re, s5_b_im, s5_c_re, s5_c_im, s5_d, s5_w_glu, s5_b_glu, s5_w_out, hg_w_in, hg_lower_bound, hg_norm_g, hg_w_out, mla_w_in, mla_q_norm_g, mla_w_q_b, mla_kv_norm_g, mla_w_kv_b, mla_w_out):
    bsz, seq, d = x.shape
    lbs = jax.nn.softmax(hg_lower_bound.astype(F32), axis=0)
    lbs = jnp.cumsum(lbs, axis=0) - lbs[0]
    h = x.reshape(bsz * seq, d)
    for layer in range(DEPTH):
        kind = layer % N_MIXERS
        slot = layer // N_MIXERS
        lg, lb = ln_mix_g[layer], ln_mix_b[layer]
        if kind == 0:
            h = _s5_mixer_ln(h, bsz, seq, s5_w_in[slot], s5_lam_re[slot], s5_lam_im[slot], s5_log_step[slot],
                             s5_b_re[slot], s5_b_im[slot], s5_c_re[slot], s5_c_im[slot], s5_d[slot],
                             s5_w_glu[slot], s5_b_glu[slot], s5_w_out[slot], lg, lb)
        elif kind == 1:
            h = _hg_mixer_ln(h, bsz, seq, hg_w_in[slot], lbs[layer], hg_norm_g[slot], hg_w_out[slot], lg, lb)
        else:
            h = _mla_mixer_ln(h, positions, bsz, seq, mla_w_in[slot], mla_q_norm_g[slot], mla_w_q_b[slot],
                              mla_kv_norm_g[slot], mla_w_kv_b[slot], mla_w_out[slot], lg, lb)
        h = _ffn_ln(h, ffn_w_in[layer], ffn_w_out[layer], ln_ffn_g[layer], ln_ffn_b[layer])
    return h.reshape(bsz, seq, d)
```

```python
import functools
import math

import jax
import jax.numpy as jnp
from jax import lax
from jax.experimental import pallas as pl
from jax.experimental.pallas import tpu as pltpu

F32 = jnp.float32
BF16 = jnp.bfloat16
LANES = 128
BF16_ROWS = 16

DEPTH = 4
N_MIXERS = 3
S5_GROUP = 16
S5_STATE = 64
S5_CHUNK = 64
S5_POST_SUB = 256
S5_PW_ROWS = 72
HG_HEAD_DIM = 128
HG_CHUNK = 128
HG_BLOCK = 2048
HG_IN_PIECE = (256, 256)
HG_IN_AHEAD = 2
MLA_HEADS = 8
MLA_NOPE = 128
MLA_ROPE = 64
MLA_V = 128
MLA_QK_PAD = 256
MLA_Q_LORA = 384
MLA_KV_LORA = 256
ROPE_THETA = 10000.0
DEEPNORM_ALPHA = (2 * DEPTH) ** 0.25
LN_EPS = 1e-5
RMS_EPS = 1e-6
FFN_CHUNK = 1024
VMEM_LIMIT = 56 * 1024 * 1024


def _params(*sem):
    return pltpu.CompilerParams(dimension_semantics=sem, vmem_limit_bytes=VMEM_LIMIT)


def _dot(a, b):
    return jnp.dot(a, b, preferred_element_type=F32)


def _dot_nt(a, b):
    return lax.dot_general(a, b, (((1,), (1,)), ((), ())), preferred_element_type=F32)


def _dot_tn(a, b):
    return lax.dot_general(a, b, (((0,), (0,)), ((), ())), preferred_element_type=F32)


def _layer_norm(y, g, b):
    mu = jnp.mean(y, axis=-1, keepdims=True)
    d = y - mu
    var = jnp.mean(d * d, axis=-1, keepdims=True)
    return d * lax.rsqrt(var + LN_EPS) * g + b


def _const_spec(shape):
    return pl.BlockSpec(shape, lambda *_: (0,) * len(shape))


def _round_robin(gens, stagger=0):
    live = list(enumerate(gens))
    rnd = 0
    while live:
        live = [(i, g) for i, g in live if rnd < i * stagger or next(g, StopIteration) is not StopIteration]
        rnd += 1


def _ffn_kernel(x_ref, win_ref, wout_ref, g_ref, b_ref, o_ref, *, hid):
    x = x_ref[...]
    xb = x.astype(BF16)
    acc = DEEPNORM_ALPHA * x
    for lo in range(0, hid, FFN_CHUNK):
        w = min(FFN_CHUNK, hid - lo)
        gate = _dot(xb, win_ref[:, lo:lo + w])
        up = _dot(xb, win_ref[:, hid + lo:hid + lo + w])
        mid = (gate * jax.nn.sigmoid(gate) * up).astype(BF16)
        acc = acc + _dot(mid, wout_ref[lo:lo + w, :])
    o_ref[...] = _layer_norm(acc, g_ref[...], b_ref[...])


def _cast_kernel(x_ref, o_ref):
    o_ref[...] = x_ref[...].astype(o_ref.dtype)


def _to_bf16(w, rows):
    n, r, c = w.shape
    spec = pl.BlockSpec((1, rows, c), lambda i, j: (i, j, 0))
    return pl.pallas_call(
        _cast_kernel,
        grid=(n, r // rows),
        in_specs=[spec],
        out_specs=spec,
        out_shape=jax.ShapeDtypeStruct(w.shape, BF16),
        compiler_params=_params("parallel", "parallel"),
        name="cast_bf16",
    )(w)


def _ffn_ln(x, w_in, w_out, g, b, tm=1024):
    t, d = x.shape
    hid = w_out.shape[0]
    return pl.pallas_call(
        functools.partial(_ffn_kernel, hid=hid),
        grid=(t // tm,),
        in_specs=[pl.BlockSpec((tm, d), lambda i: (i, 0)),
                  pl.BlockSpec((d, 2 * hid), lambda i: (0, 0), pipeline_mode=pl.Buffered(1)),
                  pl.BlockSpec((hid, d), lambda i: (0, 0), pipeline_mode=pl.Buffered(1)),
                  _const_spec((1, d)), _const_spec((1, d))],
        out_specs=pl.BlockSpec((tm, d), lambda i: (i, 0)),
        out_shape=jax.ShapeDtypeStruct((t, d), F32),
        compiler_params=_params("parallel"),
        name="ffn_ln",
    )(x, w_in.astype(BF16), w_out.astype(BF16), g.reshape(1, d), b.reshape(1, d))


def _s5_tables(lam_re, lam_im, log_step, b_re, b_im, c_re, c_im, nlev):
    L = S5_CHUNK
    lam = lax.complex(lam_re.astype(F32), lam_im.astype(F32))
    step = jnp.exp(log_step.astype(F32))[..., None]
    lam_dt = lam * step
    lam_bar = jnp.exp(lam_dt)
    b_bar = ((lam_bar - 1.0) / lam)[..., None] * lax.complex(b_re.astype(F32), b_im.astype(F32))
    g, p = lam.shape[1], lam.shape[2]
    taus = jnp.arange(S5_PW_ROWS, dtype=F32)
    pw = jnp.exp(lam_dt[:, None] * taus[None, :, None, None])
    pr, pi = jnp.real(pw), jnp.imag(pw)
    cat = lambda a, b: jnp.concatenate([a, b], axis=-1)
    pw_tab = jnp.stack([cat(pr, pr), cat(pi, pi)], axis=1)
    pw_tab = pw_tab.transpose(3, 0, 1, 2, 4).reshape(g, 4, S5_PW_ROWS, 2 * p)
    bt = jnp.swapaxes(b_bar, 2, 3)
    bre, bim = jnp.real(bt), jnp.imag(bt)
    cre, cim = c_re.astype(F32), c_im.astype(F32)
    c2 = cat(cre, -cim)
    bc_tab = jnp.stack([cat(bre, bim), cat(-bim, bre), c2, cat(-cim, -cre)], axis=2)
    bc_tab = bc_tab.transpose(1, 0, 2, 3, 4).reshape(g, 8, S5_GROUP, 2 * p)
    cmat = cat(c2[0], c2[1]).astype(BF16)
    k2 = (L * 2.0 ** jnp.arange(nlev, dtype=F32))
    a = jnp.exp(lam_dt[:, None] * k2[None, :, None, None])
    ar, ai = jnp.real(a), jnp.imag(a)
    p1 = cat(ar, ar).transpose(2, 0, 3, 1).reshape(g, 4 * p, nlev)
    p2 = cat(-ai, ai).transpose(2, 0, 3, 1).reshape(g, 4 * p, nlev)
    return pw_tab, bc_tab, cmat, p1, p2


def _s5_core_kernel(u_ref, pw_ref, bc_ref, cm_ref, p1_ref, p2_ref, y_ref, ut_sc, e_sc, gt_sc, bs_sc, tt_sc,
                    *, nc, nlev):
    L, hh, p, two_p = S5_CHUNK, S5_GROUP, S5_STATE, 2 * S5_STATE
    lh = L * hh
    n = ut_sc.shape[1]

    def blk(d, kind, tau):
        base = 4 * d + 2 * kind
        return (bc_ref[0, base] * pw_ref[0, 2 * d, tau:tau + 1, :]
                + bc_ref[0, base + 1] * pw_ref[0, 2 * d + 1, tau:tau + 1, :])

    zero = jnp.zeros((hh, two_p), BF16)
    for i in range(L):
        rows = slice(i * hh, (i + 1) * hh)
        ut_sc[rows, :] = u_ref[:, i * n:(i + 1) * n]
        e_sc[rows, :two_p] = blk(0, 0, L - 1 - i).astype(BF16)
        e_sc[rows, two_p:] = blk(1, 0, i).astype(BF16)
        gt_sc[rows, :two_p] = blk(0, 1, i + 1).astype(BF16)
        gt_sc[rows, two_p:] = blk(1, 1, L - i).astype(BF16)
    for rho in range(2 * L):
        rows = slice(rho * hh, (rho + 1) * hh)
        lag = L - 1 - rho
        bs_sc[rows, :two_p] = blk(0, 0, lag).astype(BF16) if 0 <= lag < L else zero
        bs_sc[rows, two_p:] = blk(1, 0, -lag).astype(BF16) if 0 <= -lag < L else zero
    strip = _dot_nt(cm_ref[0], bs_sc[...])
    for r in range(LANES // hh):
        rot = strip if r == 0 else pltpu.roll(strip, 2 * lh - r * hh, 1)
        for j in range(L):
            a, rr = divmod(L - 1 - j, LANES // hh)
            if rr == r:
                tt_sc[j * hh:(j + 1) * hh, :] = rot[:, LANES * a:LANES * a + lh].astype(BF16)

    ut = ut_sc[...]
    y = _dot(tt_sc[...], ut)
    s = _dot_tn(e_sc[...], ut)
    cidx = lax.broadcasted_iota(jnp.int32, (two_p, n), 1) & (nc - 1)

    def cmul(x, k, d):
        a1 = p1_ref[0, d * two_p:(d + 1) * two_p, k:k + 1]
        a2 = p2_ref[0, d * two_p:(d + 1) * two_p, k:k + 1]
        return x * a1 + jnp.concatenate([x[p:], x[:p]], axis=0) * a2

    xf, xb = s[:two_p], s[two_p:]
    for k in range(nlev):
        sh = 1 << k
        xf = xf + jnp.where(cidx >= sh, cmul(pltpu.roll(xf, sh, 1), k, 0), 0.0)
        xb = xb + jnp.where(cidx < nc - sh, cmul(pltpu.roll(xb, n - sh, 1), k, 1), 0.0)
    hf = jnp.where(cidx >= 1, pltpu.roll(xf, 1, 1), 0.0)
    hb = jnp.where(cidx < nc - 1, pltpu.roll(xb, n - 1, 1), 0.0)
    y = y + _dot(gt_sc[...], jnp.concatenate([hf, hb], axis=0).astype(BF16))
    for j in range(L):
        y_ref[:, j * n:(j + 1) * n] = y[j * hh:(j + 1) * hh, :]


def _s5_core(ut, tables, n, nc, nlev):
    d, t = ut.shape
    hh, two_p = S5_GROUP, 2 * S5_STATE
    lh = S5_CHUNK * hh
    blk = lambda a: pl.BlockSpec((1,) + a.shape[1:], lambda i: (i,) + (0,) * (a.ndim - 1))
    return pl.pallas_call(
        functools.partial(_s5_core_kernel, nc=nc, nlev=nlev),
        grid=(d // hh,),
        in_specs=[pl.BlockSpec((hh, t), lambda i: (i, 0))] + [blk(a) for a in tables],
        out_specs=pl.BlockSpec((hh, t), lambda i: (i, 0)),
        out_shape=jax.ShapeDtypeStruct((d, t), F32),
        scratch_shapes=[pltpu.VMEM((lh, n), BF16),
                        pltpu.VMEM((lh, 2 * two_p), BF16), pltpu.VMEM((lh, 2 * two_p), BF16),
                        pltpu.VMEM((2 * lh, 2 * two_p), BF16), pltpu.VMEM((lh, lh), BF16)],
        compiler_params=_params("parallel"),
        name="s5_core",
    )(ut, *tables)


def _s5_in_kernel(x_ref, wT_ref, o_ref):
    o_ref[...] = _dot_nt(wT_ref[...], x_ref[...].astype(BF16)).astype(o_ref.dtype)


def _s5_post_kernel(yT_ref, x_ref, winT_ref, d_ref, wgT_ref, bg_ref, wo_ref, g_ref, b_ref, o_ref):
    ts = S5_POST_SUB

    def sub_tile(i):
        tok = slice(i * ts, (i + 1) * ts)
        x = x_ref[tok, :]
        u = _dot_nt(winT_ref[...], x.astype(BF16))
        yield
        y = jax.nn.gelu(yT_ref[:, tok] + d_ref[...] * u)
        yield
        z = _dot(wgT_ref[...], y.astype(BF16)) + bg_ref[...]
        yield
        y = y * jax.nn.sigmoid(z)
        yield
        m = _dot_tn(y.astype(BF16), wo_ref[...])
        yield
        o_ref[tok, :] = _layer_norm(DEEPNORM_ALPHA * x + m, g_ref[...], b_ref[...])

    _round_robin([sub_tile(i) for i in range(x_ref.shape[0] // ts)], stagger=1)


def _s5_mixer_ln(h, bsz, seq, w_in, lam_re, lam_im, log_step, b_re, b_im, c_re, c_im,
                 d_skip, w_glu, b_glu, w_out, ln_g, ln_b, tm=1024):
    t, d = h.shape
    L = S5_CHUNK
    nc = seq // L
    n = bsz * nc
    nlev = max(1, (nc - 1).bit_length())
    assert nc & (nc - 1) == 0
    hp = h.reshape(n, L, d).transpose(1, 0, 2).reshape(t, d)
    w_in_t = w_in.T.astype(BF16)
    tile = pl.BlockSpec((tm, d), lambda i: (i, 0))
    tile_t = pl.BlockSpec((d, tm), lambda i: (0, i))
    ut = pl.pallas_call(
        _s5_in_kernel,
        grid=(t // tm,),
        in_specs=[tile, _const_spec((d, d))],
        out_specs=tile_t,
        out_shape=jax.ShapeDtypeStruct((d, t), BF16),
        compiler_params=_params("parallel"),
        name="s5_in",
    )(hp, w_in_t)
    tables = _s5_tables(lam_re, lam_im, log_step, b_re, b_im, c_re, c_im, nlev)
    yt = _s5_core(ut, tables, n, nc, nlev)
    row = lambda v: v.reshape(1, d)
    col = lambda v: v.astype(F32).reshape(d, 1)
    out = pl.pallas_call(
        _s5_post_kernel,
        grid=(t // tm,),
        in_specs=[tile_t, tile, _const_spec((d, d)), _const_spec((d, 1)), _const_spec((d, d)), _const_spec((d, 1)),
                  _const_spec((d, d)), _const_spec((1, d)), _const_spec((1, d))],
        out_specs=tile,
        out_shape=jax.ShapeDtypeStruct((t, d), F32),
        compiler_params=_params("parallel"),
        name="s5_post",
    )(yt, hp, w_in_t, col(d_skip), w_glu.T.astype(BF16), col(b_glu), w_out.astype(BF16), row(ln_g), row(ln_b))
    return out.reshape(L, n, d).transpose(1, 0, 2).reshape(t, d)


def _hg_masks(c):
    nlev = c.bit_length() - 1
    t = jnp.arange(c)[:, None]
    s = jnp.arange(c)[None, :]
    ms = []
    for lev in range(nlev):
        w = 1 << lev
        ms.append(((t & w) != 0) & ((s & w) == 0) & ((t >> (lev + 1)) == (s >> (lev + 1))))
    ms.append(t == s)
    fw = jnp.stack(ms).astype(F32)
    return jnp.stack([fw, fw.transpose(0, 2, 1)])


def _hg_gate(z, la2, l1m2, onem):
    zs = z * math.log2(math.e)
    ls2 = jnp.minimum(zs, 0.0) - jnp.log2(1.0 + jnp.exp2(-jnp.abs(zs)))
    b2 = l1m2 + ls2
    lf2 = jnp.maximum(la2, b2) + jnp.log2(1.0 + jnp.exp2(-jnp.abs(la2 - b2)))
    return lf2, onem * jnp.exp2(ls2 - zs)


def _hg_in_kernel(x_ref, w_ref, la_ref, l1m_ref, onem_ref, q_ref, lff_ref, lfb_ref, kf_ref, kb_ref, v_ref, g_ref,
                  y_sc):
    d = q_ref.shape[1]
    pw, ph = HG_IN_PIECE
    npc = d // pw
    nrb = x_ref.shape[0] // ph
    xs = [x_ref[r * ph:(r + 1) * ph, :].astype(BF16) for r in range(nrb)]
    piece = lambda n, r: _dot(xs[r], w_ref[:, n * pw:(n + 1) * pw])

    def finish(n, r, y):
        kind, cols, rows = n // npc, slice((n % npc) * pw, (n % npc + 1) * pw), slice(r * ph, (r + 1) * ph)
        if kind == 0:
            q_ref[rows, cols] = (y * jax.nn.sigmoid(y)).astype(q_ref.dtype)
        elif kind in (1, 2):
            lf_ref, k_ref = (lff_ref, kf_ref) if kind == 1 else (lfb_ref, kb_ref)
            lf_ref[rows, cols], k = _hg_gate(y, la_ref[:, cols], l1m_ref[:, cols], onem_ref[:, cols])
            k_ref[rows, cols] = k.astype(k_ref.dtype)
        elif kind == 3:
            v_ref[rows, cols] = y.astype(v_ref.dtype)
        else:
            g_ref[rows, cols] = y

    heavy = [(n, r) for n in range(5 * npc) if n // npc in (1, 2) for r in range(nrb)]
    light = [(n, r) for n in range(5 * npc) if n // npc not in (1, 2) for r in range(nrb)]
    order = []
    while heavy or light:
        order += heavy[:1] + light[:1]
        heavy, light = heavy[1:], light[1:]
    nslot = y_sc.shape[0]
    for i in range(len(order) + HG_IN_AHEAD):
        if i < len(order):
            y_sc[i % nslot] = piece(*order[i])
        if i >= HG_IN_AHEAD:
            finish(*order[i - HG_IN_AHEAD], y_sc[(i - HG_IN_AHEAD) % nslot])


def _hg_chunk(qb, lf, kb, vb, mask_ref, d, st_ref, emit):
    c = qb.shape[0]
    nlev = c.bit_length() - 1
    row = lax.broadcasted_iota(jnp.int32, lf.shape, 0)
    scores = mask_ref[d, nlev] * _dot_nt(qb, kb)
    p, tot = (lf if d == 0 else jnp.zeros_like(lf)), lf
    yield
    for lev in range(min(nlev, 3)):
        w = 1 << lev
        odd = (row & w) != 0
        f = jnp.exp2(jnp.where(odd, p, tot - p)).astype(BF16)
        scores = scores + mask_ref[d, lev] * _dot_nt(qb * f, kb * f)
        up = pltpu.roll(tot, w, 0)
        dn = pltpu.roll(tot, c - w, 0)
        p = p + jnp.where(odd, up, 0.0)
        tot = tot + jnp.where(odd, up, dn)
        yield
    for lev in range(3, nlev):
        w = 1 << lev
        nb = c // w
        cut = lambda x: [x[i * w:(i + 1) * w] for i in range(nb)]
        cat = lambda xs: jnp.concatenate(xs, axis=0)
        pb, tb, sb = cut(p), cut(tot), cut(scores)
        f = jnp.exp2(cat([pb[i] if i % 2 else tb[i] - pb[i] for i in range(nb)])).astype(BF16)
        q_side = lambda i: (i % 2 == 1) == (d == 0)
        if w >= BF16_ROWS:
            x = cat([a if q_side(i) else b for i, (a, b) in enumerate(zip(cut(qb), cut(kb)))]) * f
            dots = _dot_nt(x, x)
        else:
            dots = _dot_nt(qb * f, kb * f)
        db = cut(dots)
        scores = cat([sb[i] + mask_ref[d, lev, i * w:(i + 1) * w, :] * db[i] if q_side(i) else sb[i]
                      for i in range(nb)])
        p = cat([pb[i] + tb[i - 1] if i % 2 else pb[i] for i in range(nb)])
        tot = cat([tb[i] + tb[i ^ 1] for i in range(nb)])
        yield
    eq, ek = (p, tot - p) if d == 0 else (tot - p, p)
    st = st_ref[...]
    o = _dot(scores.astype(BF16), vb)
    o = o + _dot_nt(qb * jnp.exp2(eq).astype(BF16), st.astype(BF16))
    st_ref[...] = st * jnp.exp2(tot[0:1, :]) + _dot_tn(vb, kb * jnp.exp2(ek).astype(BF16))
    emit(o)


def _hg_rec_kernel(qf_ref, lff_ref, kf_ref, vf_ref, qb_ref, lfb_ref, kb_ref, vb_ref, mask_ref,
                   of_ref, ob_ref, stf_ref, stb_ref, *, nsub):
    @pl.when(pl.program_id(2) == 0)
    def _():
        stf_ref[...] = jnp.zeros_like(stf_ref)
        stb_ref[...] = jnp.zeros_like(stb_ref)

    c = HG_CHUNK

    def chain(d, rows):
        q_ref, lf_ref, k_ref, v_ref, o_ref, st_ref = ((qf_ref, lff_ref, kf_ref, vf_ref, of_ref, stf_ref) if d == 0 else
                                                      (qb_ref, lfb_ref, kb_ref, vb_ref, ob_ref, stb_ref))

        def emit(o):
            o_ref[rows, :] = o
        return _hg_chunk(q_ref[rows, :], lf_ref[rows, :], k_ref[rows, :], v_ref[rows, :], mask_ref, d, st_ref, emit)

    for j in range(nsub):
        _round_robin([chain(0, slice(j * c, (j + 1) * c)), chain(1, slice((nsub - 1 - j) * c, (nsub - j) * c))])


def _hg_post_kernel(of_ref, ob_ref, g_ref, h_ref, ng_ref, wo_ref, lg_ref, lb_ref, o_ref):
    o = of_ref[...] + ob_ref[...]
    hd = HG_HEAD_DIM
    parts = []
    for i in range(o.shape[1] // hd):
        oh = o[:, i * hd:(i + 1) * hd]
        parts.append(oh * lax.rsqrt(jnp.mean(oh * oh, axis=-1, keepdims=True) + RMS_EPS))
    on = jnp.concatenate(parts, axis=1) * ng_ref[...]
    gate = g_ref[...]
    y = on * (gate * jax.nn.sigmoid(gate))
    m = _dot(y.astype(BF16), wo_ref[...])
    o_ref[...] = _layer_norm(DEEPNORM_ALPHA * h_ref[...] + m, lg_ref[...], lb_ref[...])


def _hg_mixer_ln(h, bsz, seq, w_in, lb, norm_g, w_out, ln_g, ln_b, tm=512, tm_post=1024):
    t, d = h.shape
    hd = HG_HEAD_DIM
    nh = d // hd
    blk = min(HG_BLOCK, seq)
    nblk = seq // blk
    lb = lb.astype(F32).reshape(1, d)
    la, l1m, onem = jnp.log2(lb), jnp.log1p(-lb) * math.log2(math.e), 1.0 - lb
    row = lambda v: v.reshape(1, d)
    tile = pl.BlockSpec((tm, d), lambda i: (i, 0))
    act = lambda dt: jax.ShapeDtypeStruct((t, d), dt)
    qa, lff, lfb, kf, kb, v, gate = pl.pallas_call(
        _hg_in_kernel,
        grid=(t // tm,),
        in_specs=[tile, pl.BlockSpec(w_in.shape, lambda i: (0, 0), pipeline_mode=pl.Buffered(1)),
                  _const_spec((1, d)), _const_spec((1, d)), _const_spec((1, d))],
        out_specs=[tile] * 7,
        out_shape=[act(BF16), act(F32), act(F32), act(BF16), act(BF16), act(BF16), act(F32)],
        scratch_shapes=[pltpu.VMEM((HG_IN_AHEAD + 1, HG_IN_PIECE[1], HG_IN_PIECE[0]), F32)],
        compiler_params=_params("parallel"),
        name="hg_in",
    )(h, w_in.astype(BF16), la, l1m, onem)
    masks = _hg_masks(HG_CHUNK)
    fw = pl.BlockSpec((blk, hd), lambda b, hh, i: (b * nblk + i, hh))
    bw = pl.BlockSpec((blk, hd), lambda b, hh, i: (b * nblk + nblk - 1 - i, hh))
    o_fw, o_bw = pl.pallas_call(
        functools.partial(_hg_rec_kernel, nsub=blk // HG_CHUNK),
        grid=(bsz, nh, nblk),
        in_specs=[fw, fw, fw, fw, bw, bw, bw, bw, _const_spec(masks.shape)],
        out_specs=[fw, bw],
        out_shape=[act(F32)] * 2,
        scratch_shapes=[pltpu.VMEM((hd, hd), F32), pltpu.VMEM((hd, hd), F32)],
        compiler_params=_params("parallel", "parallel", "arbitrary"),
        name="hg_rec",
    )(qa, lff, kf, v, qa, lfb, kb, v, masks)

    tile = pl.BlockSpec((tm_post, d), lambda i: (i, 0))
    return pl.pallas_call(
        _hg_post_kernel,
        grid=(t // tm_post,),
        in_specs=[tile, tile, tile, tile, _const_spec((1, d)),
                  _const_spec((d, d)), _const_spec((1, d)), _const_spec((1, d))],
        out_specs=tile,
        out_shape=act(F32),
        compiler_params=_params("parallel"),
        name="hg_post",
    )(o_fw, o_bw, gate, h, jnp.tile(norm_g.astype(F32), nh).reshape(1, d), w_out.astype(BF16), row(ln_g), row(ln_b))


def _mla_proj_kernel(x_ref, pos_ref, winT_ref, gq_ref, gkv_ref, wqT_ref, wkvT_ref, invf_ref,
                     qT_ref, k_ref, vT_ref):
    ql, kvl, half = MLA_Q_LORA, MLA_KV_LORA, MLA_ROPE // 2
    dq = MLA_NOPE + MLA_ROPE
    lat = _dot_nt(winT_ref[...], x_ref[...].astype(BF16))
    q_lat, kv_lat, k_rope = lat[:ql], lat[ql:ql + kvl], lat[ql + kvl:]

    def rms(v, g):
        return (v * lax.rsqrt(jnp.mean(v * v, axis=0, keepdims=True) + RMS_EPS) * g).astype(BF16)

    q = _dot(wqT_ref[...], rms(q_lat, gq_ref[...]))
    kv = _dot(wkvT_ref[...], rms(kv_lat, gkv_ref[...]))
    ang = invf_ref[...] * pos_ref[0].astype(F32)
    cos, sin = jnp.cos(ang), jnp.sin(ang)

    def rope(t1, t2):
        return t1 * cos - t2 * sin, t1 * sin + t2 * cos

    scale = dq ** -0.5 * math.log2(math.e)
    ones_rows = (lax.broadcasted_iota(jnp.int32, (BF16_ROWS, lat.shape[1]), 0) == 0).astype(BF16)
    k1, k2 = rope(k_rope[:half], k_rope[half:])
    pad = jnp.zeros((MLA_QK_PAD - dq, lat.shape[1]), F32)
    k_tail = jnp.concatenate([k1, k2, pad], axis=0).T.astype(BF16)
    for hh in range(MLA_HEADS):
        qh = q[hh * dq:(hh + 1) * dq]
        q1, q2 = rope(qh[MLA_NOPE:MLA_NOPE + half], qh[MLA_NOPE + half:])
        qT_ref[0, hh, 0:MLA_NOPE, :] = (qh[:MLA_NOPE] * scale).astype(BF16)
        qT_ref[0, hh, MLA_NOPE:MLA_NOPE + half, :] = (q1 * scale).astype(BF16)
        qT_ref[0, hh, MLA_NOPE + half:dq, :] = (q2 * scale).astype(BF16)
        qT_ref[0, hh, dq:, :] = pad.astype(BF16)
        kvh = kv[hh * (MLA_NOPE + MLA_V):(hh + 1) * (MLA_NOPE + MLA_V)]
        k_ref[0, hh, 0, :, 0:MLA_NOPE] = kvh[:MLA_NOPE].T.astype(BF16)
        k_ref[0, hh, 0, :, MLA_NOPE:] = k_tail
        vT_ref[0, hh, 0, 0:MLA_V, :] = kvh[MLA_NOPE:].astype(BF16)
        vT_ref[0, hh, 0, MLA_V:, :] = ones_rows


def _attn_kernel(qT_ref, k_ref, vT_ref, o_ref, s0_sc, s1_sc, t0_sc, t1_sc, m_sc, acc_sc, *, nkv, tqs, tks):
    dv = MLA_V
    s_sc, t_sc = (s0_sc, s1_sc), (t0_sc, t1_sc)
    tk, tq = s0_sc.shape
    m_sc[...] = jnp.full_like(m_sc, -jnp.inf)
    acc_sc[...] = jnp.zeros_like(acc_sc)

    def scores(j, par, c):
        cols = slice(c * tqs, (c + 1) * tqs)
        s = _dot(k_ref[0, 0, j], qT_ref[0, 0, :, cols])
        s_sc[par][:, cols] = s
        t_sc[par][:, cols] = jnp.max(s, axis=0, keepdims=True)

    def softmax_pv(j, par, c):
        cols = slice(c * tqs, (c + 1) * tqs)
        m_prev = m_sc[:, cols]
        m_new = jnp.maximum(m_prev, t_sc[par][:, cols])
        acc = jnp.exp2(m_prev - m_new) * acc_sc[:, cols]
        for r in range(tk // tks):
            rows = slice(r * tks, (r + 1) * tks)
            p = jnp.exp2(s_sc[par][rows, cols] - m_new).astype(BF16)
            acc = acc + _dot(vT_ref[0, 0, j, :, rows], p)
        acc_sc[:, cols] = acc
        m_sc[:, cols] = m_new

    nsub = tq // tqs

    def step(j_scores, j_soft):
        for c in range(nsub):
            if j_scores is not None:
                scores(j_scores[0], j_scores[1], c)
            if j_soft is not None:
                softmax_pv(j_soft[0], j_soft[1], c)

    step((0, 0), None)

    def body(jj, carry):
        j = 2 * jj
        step((j + 1, 1), (j, 0))
        step((j + 2, 0), (j + 1, 1))
        return carry

    lax.fori_loop(0, nkv // 2 - 1, body, 0, unroll=3)
    step((nkv - 1, 1), (nkv - 2, 0))
    step(None, (nkv - 1, 1))
    acc = acc_sc[...]
    o_ref[0, 0] = (acc[:dv] / acc[dv:dv + 1]).astype(o_ref.dtype)


def _mla_out_kernel(oT_ref, h_ref, wo_ref, g_ref, b_ref, o_ref):
    oT = oT_ref[0]
    oT = oT.reshape(oT.shape[0] * oT.shape[1], oT.shape[2])
    m = _dot_tn(oT, wo_ref[...])
    o_ref[...] = _layer_norm(DEEPNORM_ALPHA * h_ref[...] + m, g_ref[...], b_ref[...])


def _mla_mixer_ln(h, positions, bsz, seq, w_in, q_norm_g, w_q_b, kv_norm_g, w_kv_b, w_out, ln_g, ln_b,
                  tm=512, tq=4096, tqs=256, tks=256, tm_out=1024):
    t, d = h.shape
    nh, dq, dv = MLA_HEADS, MLA_NOPE + MLA_ROPE, MLA_V
    half = MLA_ROPE // 2
    inv_freq = 1.0 / (ROPE_THETA ** (jnp.arange(half, dtype=F32) * (2.0 / MLA_ROPE)))
    nlat = w_in.shape[1]
    nt = seq // tm
    dp = MLA_QK_PAD
    qT, kc, vT = pl.pallas_call(
        _mla_proj_kernel,
        grid=(bsz, nt),
        in_specs=[pl.BlockSpec((tm, d), lambda b, i: (b * nt + i, 0)),
                  pl.BlockSpec((1, 1, tm), lambda b, i: (b, 0, i)),
                  _const_spec((nlat, d)), _const_spec((MLA_Q_LORA, 1)), _const_spec((MLA_KV_LORA, 1)),
                  _const_spec((nh * dq, MLA_Q_LORA)), _const_spec((nh * (MLA_NOPE + dv), MLA_KV_LORA)),
                  _const_spec((half, 1))],
        out_specs=[pl.BlockSpec((1, nh, dp, tm), lambda b, i: (b, 0, 0, i)),
                   pl.BlockSpec((1, nh, 1, tm, dp), lambda b, i: (b, 0, i, 0, 0)),
                   pl.BlockSpec((1, nh, 1, dv + BF16_ROWS, tm), lambda b, i: (b, 0, i, 0, 0))],
        out_shape=[jax.ShapeDtypeStruct((bsz, nh, dp, seq), BF16),
                   jax.ShapeDtypeStruct((bsz, nh, nt, tm, dp), BF16),
                   jax.ShapeDtypeStruct((bsz, nh, nt, dv + BF16_ROWS, tm), BF16)],
        compiler_params=_params("parallel", "parallel"),
        name="mla_proj",
    )(h, positions.reshape(bsz, 1, seq), w_in.T.astype(BF16), q_norm_g.astype(F32).reshape(-1, 1),
      kv_norm_g.astype(F32).reshape(-1, 1), w_q_b.T.astype(BF16), w_kv_b.T.astype(BF16),
      inv_freq.reshape(half, 1))

    tk, nkv = tm, nt
    tq = min(tq, seq)
    oT = pl.pallas_call(
        functools.partial(_attn_kernel, nkv=nkv, tqs=tqs, tks=tks),
        grid=(bsz, nh, seq // tq),
        in_specs=[pl.BlockSpec((1, 1, dp, tq), lambda b, hh, i: (b, hh, 0, i)),
                  pl.BlockSpec((1, 1, nkv, tk, dp), lambda b, hh, i: (b, hh, 0, 0, 0),
                               pipeline_mode=pl.Buffered(1)),
                  pl.BlockSpec((1, 1, nkv, dv + BF16_ROWS, tk), lambda b, hh, i: (b, hh, 0, 0, 0),
                               pipeline_mode=pl.Buffered(1))],
        out_specs=pl.BlockSpec((1, 1, dv, tq), lambda b, hh, i: (b, hh, 0, i)),
        out_shape=jax.ShapeDtypeStruct((bsz, nh, dv, seq), BF16),
        scratch_shapes=[pltpu.VMEM((tk, tq), F32), pltpu.VMEM((tk, tq), F32),
                        pltpu.VMEM((1, tq), F32), pltpu.VMEM((1, tq), F32),
                        pltpu.VMEM((1, tq), F32), pltpu.VMEM((dv + BF16_ROWS, tq), F32)],
        compiler_params=_params("parallel", "parallel", "arbitrary"),
        name="mla_attn",
    )(qT, kc, vT)

    to = min(tm_out, seq)
    no = seq // to
    return pl.pallas_call(
        _mla_out_kernel,
        grid=(bsz, no),
        in_specs=[pl.BlockSpec((1, nh, dv, to), lambda b, i: (b, 0, 0, i)),
                  pl.BlockSpec((to, d), lambda b, i: (b * no + i, 0)),
                  _const_spec((nh * dv, d)), _const_spec((1, d)), _const_spec((1, d))],
        out_specs=pl.BlockSpec((to, d), lambda b, i: (b * no + i, 0)),
        out_shape=jax.ShapeDtypeStruct((t, d), F32),
        compiler_params=_params("parallel", "parallel"),
        name="mla_out",
    )(oT, h, w_out.astype(BF16), ln_g.reshape(1, d), ln_b.reshape(1, d))


def kernel(x, positions, ln_mix_g, ln_mix_b, ln_ffn_g, ln_ffn_b, ffn_w_in, ffn_w_out, s5_w_in, s5_lam_re, s5_lam_im, s5_log_step, s5_b_re, s5_b_im, s5_c_re, s5_c_im, s5_d, s5_w_glu, s5_b_glu, s5_w_out, hg_w_in, hg_lower_bound, hg_norm_g, hg_w_out, mla_w_in, mla_q_norm_g, mla_w_q_b, mla_kv_norm_g, mla_w_kv_b, mla_w_out):
    bsz, seq, d = x.shape
    lbs = jax.nn.softmax(hg_lower_bound.astype(F32), axis=0)
    lbs = jnp.cumsum(lbs, axis=0) - lbs[0]
    h = x.reshape(bsz * seq, d)
    ffn_w_in = _to_bf16(ffn_w_in, ffn_w_in.shape[1] // 2)
    ffn_w_out = _to_bf16(ffn_w_out, ffn_w_out.shape[1] // 2)
    for layer in range(DEPTH):
        kind = layer % N_MIXERS
        slot = layer // N_MIXERS
        lg, lb = ln_mix_g[layer], ln_mix_b[layer]
        if kind == 0:
            h = _s5_mixer_ln(h, bsz, seq, s5_w_in[slot], s5_lam_re[slot], s5_lam_im[slot], s5_log_step[slot],
                             s5_b_re[slot], s5_b_im[slot], s5_c_re[slot], s5_c_im[slot], s5_d[slot],
                             s5_w_glu[slot], s5_b_glu[slot], s5_w_out[slot], lg, lb)
        elif kind == 1:
            h = _hg_mixer_ln(h, bsz, seq, hg_w_in[slot], lbs[layer], hg_norm_g[slot], hg_w_out[slot], lg, lb)
        else:
            h = _mla_mixer_ln(h, positions, bsz, seq, mla_w_in[slot], mla_q_norm_g[slot], mla_w_q_b[slot],
                              mla_kv_norm_g[slot], mla_w_kv_b[slot], mla_w_out[slot], lg, lb)
        h = _ffn_ln(h, ffn_w_in[layer], ffn_w_out[layer], ln_ffn_g[layer], ln_ffn_b[layer])
    return h.reshape(bsz, seq, d)
```
